```python
import jax, jax.numpy as jnp
from jax import lax
import numpy as np

D_MODEL = 1024
BATCH = 4
SEQ = 8192
DEPTH = 2

RWKV_HEADS = 8
RWKV_HEAD_DIM = 64
RWKV_WIDTH = RWKV_HEADS * RWKV_HEAD_DIM
DECAY_LORA = 64
ICLR_LORA = 64
GATE_LORA = 128
ATTN_HEADS = 8
ATTN_HEAD_DIM = 64
ATTN_WIDTH = ATTN_HEADS * ATTN_HEAD_DIM
MOBA_BLOCK = 256
MOBA_TOP_K = 3
Q_CHUNK = 128
POOL_WINDOWS = (2, 4, 8, 16)
POOL_GROUPS = 4
POOL_GROUP_DIM = 128
POOL_WIDTH = POOL_GROUPS * POOL_GROUP_DIM
N_BRANCH = 3
D_FF = 4 * D_MODEL
RMS_EPS = 1e-6
GN_EPS = 64e-5
NEG_INF = -1e30

RWKV_IN = 3 * RWKV_WIDTH + DECAY_LORA + ICLR_LORA + GATE_LORA
ATTN_IN = 3 * ATTN_WIDTH
GATE_IN = N_BRANCH * D_MODEL
N_IN = RWKV_IN + ATTN_IN + POOL_WIDTH + GATE_IN

kernel_name = 'hybrid_rwkv7_moba_pool_gated_block'


def rms_norm(x, g):
    x = x.astype(jnp.float32)
    return x * lax.rsqrt(jnp.mean(x * x, axis=-1, keepdims=True) + RMS_EPS) * g.astype(jnp.float32)


def token_shift(p):
    return jnp.pad(p, ((0, 0), (1, 0), (0, 0)))[:, :-1]


def rwkv7_time_mix(p, mu, w0, w_up, a0, a_up, g_up, k_k, k_a, r_k, lnx_g, lnx_b):
    B, T, _ = p.shape
    H, N, W = RWKV_HEADS, RWKV_HEAD_DIM, RWKV_WIDTH
    p = p + (token_shift(p) - p) * mu
    o3 = 3 * W
    o4 = o3 + DECAY_LORA
    o5 = o4 + ICLR_LORA
    r, k, v = p[..., :W], p[..., W:2 * W], p[..., 2 * W:o3]
    xw, xa, xg = p[..., o3:o4], p[..., o4:o5], p[..., o5:]
    w_log = -jax.nn.softplus(-(w0 + jnp.tanh(xw) @ w_up)) - 0.5
    decay = jnp.exp(-jnp.exp(w_log))
    a = jax.nn.sigmoid(a0 + xa @ a_up)
    g = jax.nn.sigmoid(xg) @ g_up
    heads = lambda t: t.reshape(B, T, H, N)
    kk = heads(k * k_k)
    kk = kk / jnp.maximum(jnp.linalg.norm(kk, axis=-1, keepdims=True), 1e-12)
    k = k * (1.0 + (a - 1.0) * k_a)
    r, decay, k, v, a = heads(r), heads(decay), heads(k), heads(v), heads(a)

    def step(S, inp):
        r_t, w_t, k_t, v_t, a_t, b_t = inp
        sa = jnp.einsum('bhvk,bhk->bhv', S, a_t)
        S = S * w_t[:, :, None, :] + sa[..., None] * b_t[:, :, None, :] + v_t[..., None] * k_t[:, :, None, :]
        return S, jnp.einsum('bhvk,bhk->bhv', S, r_t)

    xs = tuple(jnp.swapaxes(t, 0, 1) for t in (r, decay, k, v, -kk, kk * a))
    S0 = jnp.zeros((B, H, N, N), jnp.float32)
    _, o = lax.scan(step, S0, xs)
    o = jnp.swapaxes(o, 0, 1)
    mean = jnp.mean(o, axis=-1, keepdims=True)
    var = jnp.mean(jnp.square(o - mean), axis=-1, keepdims=True)
    o = ((o - mean) * lax.rsqrt(var + GN_EPS)).reshape(B, T, W) * lnx_g + lnx_b
    bonus = jnp.sum(r * k * r_k, axis=-1, keepdims=True) * v
    return (o + bonus.reshape(B, T, W)) * g


def moba_attention(q, k, v):
    B, T, H, Dh = q.shape
    NB = -(-T // MOBA_BLOCK)
    Tp = NB * MOBA_BLOCK
    NC = Tp // Q_CHUNK
    ks = min(MOBA_TOP_K, NB)
    pad = ((0, 0), (0, Tp - T), (0, 0), (0, 0))
    q, k, v = (jnp.pad(t, pad).transpose(0, 2, 1, 3) for t in (q, k, v))
    kb = k.reshape(B, H, NB, MOBA_BLOCK, Dh)
    vb = v.reshape(B, H, NB, MOBA_BLOCK, Dh)
    k_mean = jnp.mean(kb, axis=3)
    gate = jnp.einsum('bhtd,bhnd->bhtn', q, k_mean)
    q_blk = jnp.arange(Tp) // MOBA_BLOCK
    past = jnp.arange(NB)[None, :] < q_blk[:, None]
    gate = jnp.where(past, gate, NEG_INF)
    _, sel = lax.top_k(gate, ks)
    valid = sel < q_blk[:, None]
    slopes = jnp.asarray(2.0 ** (-8.0 * np.arange(1, H + 1) / H), jnp.float32)
    scale = Dh ** -0.5
    offs = jnp.arange(MOBA_BLOCK)
    head_idx = jnp.arange(H)[:, None, None]

    def per_batch(args):
        qb, kbb, vbb, selb, validb = args

        def to_chunks(t):
            return jnp.swapaxes(t.reshape(H, NC, Q_CHUNK, *t.shape[2:]), 0, 1)

        def chunk(cargs):
            c, qc, sc, vc = cargs
            t_pos = c * Q_CHUNK + jnp.arange(Q_CHUNK)
            own = (c * Q_CHUNK) // MOBA_BLOCK
            k_own = lax.dynamic_index_in_dim(kbb, own, axis=1, keepdims=False)
            v_own = lax.dynamic_index_in_dim(vbb, own, axis=1, keepdims=False)
            k_sel = kbb[head_idx, sc]
            v_sel = vbb[head_idx, sc]
            dist_own = (t_pos[:, None] - (own * MOBA_BLOCK + offs)[None, :]).astype(jnp.float32)
            dist_sel = (t_pos[None, :, None, None] - (sc[..., None] * MOBA_BLOCK + offs)).astype(jnp.float32)
            l_own = jnp.einsum('hcd,hsd->hcs', qc, k_own) * scale - slopes[:, None, None] * dist_own
            l_own = jnp.where(dist_own >= 0, l_own, NEG_INF)
            l_sel = jnp.einsum('hcd,hcjsd->hcjs', qc, k_sel) * scale - slopes[:, None, None, None] * dist_sel
            l_sel = jnp.where(vc[..., None], l_sel, NEG_INF)
            logits = jnp.concatenate([l_sel.reshape(H, Q_CHUNK, ks * MOBA_BLOCK), l_own], axis=-1)
            prob = jax.nn.softmax(logits, axis=-1)
            p_sel = prob[..., :ks * MOBA_BLOCK].reshape(H, Q_CHUNK, ks, MOBA_BLOCK)
            p_own = prob[..., ks * MOBA_BLOCK:]
            return jnp.einsum('hcs,hsd->hcd', p_own, v_own) + jnp.einsum('hcjs,hcjsd->hcd', p_sel, v_sel)

        out = lax.map(chunk, (jnp.arange(NC), to_chunks(qb), to_chunks(selb), to_chunks(validb)))
        return jnp.swapaxes(out, 0, 1).reshape(H, Tp, Dh)

    out = lax.map(per_batch, (q, kb, vb, sel, valid))
    return out.transpose(0, 2, 1, 3)[:, :T].reshape(B, T, H * Dh)


def pool_mix(u, pool_w, pool_scale):
    B, T, _ = u.shape
    ug = u.reshape(B, T, POOL_GROUPS, POOL_GROUP_DIM)
    cs = jnp.cumsum(ug, axis=1)
    outs = []
    for gi, w in enumerate(POOL_WINDOWS):
        c = cs[:, :, gi]
        lo = jnp.pad(c[:, :T - w], ((0, 0), (w, 0), (0, 0)))
        count = jnp.minimum(jnp.arange(T) + 1, w).astype(jnp.float32)[None, :, None]
        outs.append((c - lo) / count - ug[:, :, gi])
    d = jnp.stack(outs, axis=2)
    y = jnp.einsum('btgc,gcd->btgd', d, pool_w).reshape(B, T, POOL_WIDTH)
    return y * pool_scale


def setup_inputs(seed: int = 0) -> dict:
    key = jax.random.key(seed)
    ks = jax.random.split(key, 24)
    L, D, W = DEPTH, D_MODEL, RWKV_WIDTH
    nrm = lambda k, s: jax.random.normal(k, s, jnp.float32)
    return {
        'x': nrm(ks[0], (BATCH, SEQ, D)),
        'norm1_g': 1.0 + 0.05 * nrm(ks[1], (L, D)),
        'w_in': nrm(ks[2], (L, D, N_IN)) * D ** -0.5,
        'shift_mu': jax.random.uniform(ks[3], (L, RWKV_IN), jnp.float32),
        'w0': jax.random.uniform(ks[4], (L, W), jnp.float32, minval=-6.0, maxval=-1.0),
        'w_up': nrm(ks[5], (L, DECAY_LORA, W)) * 0.5 * DECAY_LORA ** -0.5,
        'a0': 0.5 * nrm(ks[6], (L, W)),
        'a_up': nrm(ks[7], (L, ICLR_LORA, W)) * 0.5 * ICLR_LORA ** -0.5,
        'g_up': nrm(ks[8], (L, GATE_LORA, W)) * GATE_LORA ** -0.5,
        'k_k': 0.85 + 0.05 * nrm(ks[9], (L, W)),
        'k_a': 1.0 + 0.05 * nrm(ks[10], (L, W)),
        'r_k': 0.1 * nrm(ks[11], (L, RWKV_HEADS, RWKV_HEAD_DIM)),
        'lnx_g': 1.0 + 0.05 * nrm(ks[12], (L, W)),
        'lnx_b': 0.01 * nrm(ks[13], (L, W)),
        'qn_g': 1.0 + 0.05 * nrm(ks[14], (L, ATTN_HEAD_DIM)),
        'kn_g': 1.0 + 0.05 * nrm(ks[15], (L, ATTN_HEAD_DIM)),
        'pool_w': nrm(ks[16], (L, POOL_GROUPS, POOL_GROUP_DIM, POOL_GROUP_DIM)) * POOL_GROUP_DIM ** -0.5,
        'pool_scale': 1.0 + 0.1 * nrm(ks[17], (L, POOL_WIDTH)),
        'w_branch': nrm(ks[18], (L, N_BRANCH, W, D)) * W ** -0.5,
        'w_out': nrm(ks[19], (L, D, D)) * D ** -0.5,
        'norm2_g': 1.0 + 0.05 * nrm(ks[20], (L, D)),
        'w_ff1': nrm(ks[21], (L, D, D_FF)) * D ** -0.5,
        'w_ff2': nrm(ks[22], (L, D_FF, D)) * D_FF ** -0.5,
    }


def reference(x, norm1_g, w_in, shift_mu, w0, w_up, a0, a_up, g_up, k_k, k_a, r_k, lnx_g, lnx_b,
              qn_g, kn_g, pool_w, pool_scale, w_branch, w_out, norm2_g, w_ff1, w_ff2):
    B, T, _ = x.shape
    out_dtype = x.dtype
    x = x.astype(jnp.float32)
    o1 = RWKV_IN
    o2 = o1 + ATTN_IN
    o3 = o2 + POOL_WIDTH
    for l in range(DEPTH):
        h = rms_norm(x, norm1_g[l])
        proj = h @ w_in[l]
        p_rwkv, p_attn, p_pool, p_gate = proj[..., :o1], proj[..., o1:o2], proj[..., o2:o3], proj[..., o3:]
        y_rwkv = rwkv7_time_mix(p_rwkv, shift_mu[l], w0[l], w_up[l], a0[l], a_up[l], g_up[l],
                                k_k[l], k_a[l], r_k[l], lnx_g[l], lnx_b[l])
        q = p_attn[..., :ATTN_WIDTH].reshape(B, T, ATTN_HEADS, ATTN_HEAD_DIM)
        k = p_attn[..., ATTN_WIDTH:2 * ATTN_WIDTH].reshape(B, T, ATTN_HEADS, ATTN_HEAD_DIM)
        v = p_attn[..., 2 * ATTN_WIDTH:].reshape(B, T, ATTN_HEADS, ATTN_HEAD_DIM)
        y_attn = moba_attention(rms_norm(q, qn_g[l]), rms_norm(k, kn_g[l]), v)
        y_pool = pool_mix(p_pool, pool_w[l], pool_scale[l])
        ys = jnp.stack([y_rwkv, y_attn, y_pool], axis=2)
        branch = jnp.einsum('btnc,ncd->btnd', ys, w_branch[l])
        gates = jax.nn.sigmoid(p_gate.reshape(B, T, N_BRANCH, D_MODEL))
        x = x + jnp.sum(gates * branch, axis=2) @ w_out[l]
        h2 = rms_norm(x, norm2_g[l])
        x = x + jnp.square(jax.nn.relu(h2 @ w_ff1[l])) @ w_ff2[l]
    return x.astype(out_dtype)
```

```python
import functools

import numpy as np
import jax
import jax.numpy as jnp
from jax import lax
from jax.experimental import pallas as pl
from jax.experimental.pallas import tpu as pltpu

F32 = jnp.float32
BF16 = jnp.bfloat16

D_MODEL = 1024
RWKV_HEADS = 8
HEAD_DIM = 64
WIDTH = RWKV_HEADS * HEAD_DIM
DECAY_LORA = 64
ICLR_LORA = 64
GATE_LORA = 128
ATTN_HEADS = 8
MOBA_BLOCK = 256
MOBA_TOP_K = 3
POOL_WINDOWS = (2, 4, 8, 16)
POOL_GROUP_DIM = 128
N_BRANCH = 3
D_FF = 4 * D_MODEL
RMS_EPS = 1e-6
GN_EPS = 64e-5
NEG_INF = -1e30

RWKV_IN = 3 * WIDTH + DECAY_LORA + ICLR_LORA + GATE_LORA
RWKV_PAD = 2048
ATTN_COL = RWKV_PAD // WIDTH
POOL_COL = ATTN_COL + 3
GATE_COL = (RWKV_PAD + 4 * WIDTH) // D_MODEL
N_PROJ = RWKV_PAD + 4 * WIDTH + N_BRANCH * D_MODEL

CHUNK = 64
VMEM_LIMIT = 56 * 1024 * 1024

NN = (((1,), (0,)), ((), ()))
NT = (((1,), (1,)), ((), ()))
TN = (((0,), (0,)), ((), ()))


def _dot(a, b, dims=NN):
    return lax.dot_general(a, b, dims, preferred_element_type=F32)


def _split(a):
    hi = a.astype(BF16)
    lo = (a - hi.astype(F32)).astype(BF16)
    return hi, lo


def _dot3(a, b, dims=NN):
    ah, al = _split(a)
    bh, bl = _split(b)
    return _dot(ah, bh, dims) + (_dot(ah, bl, dims) + _dot(al, bh, dims))


def _dot2(a, b_bf16, dims=NN):
    ah, al = _split(a)
    return _dot(ah, b_bf16, dims) + _dot(al, b_bf16, dims)


def _params(sem):
    return pltpu.CompilerParams(dimension_semantics=sem, vmem_limit_bytes=VMEM_LIMIT)


def _inproj_kernel(x_ref, g_ref, w_ref, o_ref, h_scr):
    @pl.when(pl.program_id(1) == 0)
    def _():
        x = x_ref[...]
        ms = jnp.mean(x * x, axis=-1, keepdims=True)
        h_scr[...] = (x * lax.rsqrt(ms + RMS_EPS) * g_ref[...]).astype(BF16)

    o_ref[...] = _dot(h_scr[...], w_ref[...])


def _inproj(x2, g, w_pad, tm=1024, tn=1024):
    m = x2.shape[0]
    tm = min(tm, m)
    return pl.pallas_call(
        _inproj_kernel,
        grid=(m // tm, N_PROJ // tn),
        in_specs=[
            pl.BlockSpec((tm, D_MODEL), lambda i, j: (i, 0)),
            pl.BlockSpec((1, D_MODEL), lambda i, j: (0, 0)),
            pl.BlockSpec((D_MODEL, tn), lambda i, j: (0, j)),
        ],
        out_specs=pl.BlockSpec((tm, tn), lambda i, j: (i, j)),
        out_shape=jax.ShapeDtypeStruct((m, N_PROJ), F32),
        scratch_shapes=[pltpu.VMEM((tm, D_MODEL), BF16)],
        compiler_params=_params(("parallel", "arbitrary")),
        name="inproj",
    )(x2, g, w_pad)


def _unit_lower_inverse(n_mat, row, col):
    eye = (row == col).astype(F32)
    blk16 = (row >> 4) == (col >> 4)
    blk32 = (row >> 5) == (col >> 5)
    d = jnp.where(blk16, n_mat, 0.0)
    x = eye + d
    d2 = _dot3(d, d)
    x = x + _dot3(d2, x)
    d4 = _dot3(d2, d2)
    x = x + _dot3(d4, x)
    d8 = _dot3(d4, d4)
    x = x + _dot3(d8, x)
    e1 = jnp.where(blk32 & jnp.logical_not(blk16), n_mat, 0.0)
    x = x + _dot3(x, _dot3(e1, x))
    e2 = jnp.where(jnp.logical_not(blk32), n_mat, 0.0)
    x = x + _dot3(x, _dot3(e2, x))
    return x


def _rwkv_kernel(p_ref, prev_ref, mu_ref, w0_ref, wup_ref, a0_ref, aup_ref, gup_ref,
                 kk_ref, ka_ref, rk_ref, lng_ref, lnb_ref, seg_ref, y_ref, s_scr):
    c = pl.program_id(1)
    L = CHUNK
    W = WIDTH

    @pl.when(c == 0)
    def _():
        s_scr[...] = jnp.zeros_like(s_scr)

    p = p_ref[...]
    prev_row = jnp.where(c == 0, 0.0, prev_ref[7:8, :])
    row_p = lax.broadcasted_iota(jnp.int32, p.shape, 0)
    shifted = jnp.where(row_p == 0, prev_row, pltpu.roll(p, 1, axis=0))
    ps = p + (shifted - p) * mu_ref[...]

    r = ps[:, 0:W]
    k = ps[:, W:2 * W]
    v = ps[:, 2 * W:3 * W]
    lo = ps[:, 3 * W:3 * W + 128]
    xg = ps[:, 3 * W + 128:3 * W + 256]

    wpre = w0_ref[...] + _dot3(jnp.tanh(lo), wup_ref[...])
    z = -wpre
    softplus = jnp.maximum(z, 0.0) + jnp.log1p(jnp.exp(-jnp.abs(z)))
    logw = -jnp.exp(-softplus - 0.5)
    a = jax.nn.sigmoid(a0_ref[...] + _dot3(lo, aup_ref[...]))
    g = _dot3(jax.nn.sigmoid(xg), gup_ref[...])

    kk = k * kk_ref[...]
    ss = _dot2(kk * kk, seg_ref[...])
    kk = kk / jnp.maximum(jnp.sqrt(ss), 1e-12)
    k2 = k * (1.0 + (a - 1.0) * ka_ref[...])
    avec = -kk
    bvec = kk * a

    row = lax.broadcasted_iota(jnp.int32, (L, W), 0)
    cl = logw
    s = 1
    while s < L:
        cl = cl + jnp.where(row >= s, pltpu.roll(cl, s, axis=0), 0.0)
        s *= 2
    cl_last = cl[L - 1:L, :]
    rt = r * jnp.exp(cl)
    at = avec * jnp.exp(cl - logw)
    e_neg = jnp.exp(-cl)
    kt = k2 * e_neg
    bt = bvec * e_neg
    e_rem = jnp.exp(cl_last - cl)
    kh = k2 * e_rem
    bh = bvec * e_rem
    p_last = jnp.exp(cl_last)
    rkk = r * k2 * rk_ref[...]

    row64 = lax.broadcasted_iota(jnp.int32, (L, L), 0)
    col64 = lax.broadcasted_iota(jnp.int32, (L, L), 1)
    strict = row64 > col64
    incl = row64 >= col64

    for h in range(RWKV_HEADS):
        sl = slice(h * HEAD_DIM, (h + 1) * HEAD_DIM)
        ar = jnp.concatenate([at[:, sl], rt[:, sl]], axis=0)
        mk = _dot3(ar, kt[:, sl], NT)
        mb = _dot3(ar, bt[:, sl], NT)
        m_ak = jnp.where(strict, mk[:L], 0.0)
        m_rk = jnp.where(incl, mk[L:], 0.0)
        m_ab = jnp.where(strict, mb[:L], 0.0)
        m_rb = jnp.where(incl, mb[L:], 0.0)
        tinv = _unit_lower_inverse(m_ab, row64, col64)
        s0 = s_scr[h]
        ars = _dot3(ar, s0, NT)
        vh = v[:, sl]
        u = _dot3(tinv, ars[:L] + _dot3(m_ak, vh))
        o = ars[L:] + _dot3(m_rk, vh) + _dot3(m_rb, u)
        s_scr[h] = s0 * p_last[:, sl] + _dot3(vh, kh[:, sl], TN) + _dot3(u, bh[:, sl], TN)

        mean = jnp.mean(o, axis=-1, keepdims=True)
        var = jnp.mean(jnp.square(o - mean), axis=-1, keepdims=True)
        on = (o - mean) * lax.rsqrt(var + GN_EPS) * lng_ref[:, sl] + lnb_ref[:, sl]
        bonus = jnp.sum(rkk[:, sl], axis=-1, keepdims=True) * vh
        y_ref[:, sl] = (on + bonus) * g[:, sl]


def _rwkv(proj, B, T, mu, w0, wup_pad, a0, aup_pad, gup, k_k, k_a, r_k, lng, lnb, seg):
    L = CHUNK
    nc = T // L
    vec = lambda n: pl.BlockSpec((1, n), lambda b, c: (0, 0))
    mat = lambda m, n: pl.BlockSpec((m, n), lambda b, c: (0, 0))
    return pl.pallas_call(
        _rwkv_kernel,
        grid=(B, nc),
        in_specs=[
            pl.BlockSpec((L, RWKV_IN), lambda b, c: (b * nc + c, 0)),
            pl.BlockSpec((8, RWKV_IN), lambda b, c: (jnp.maximum((b * nc + c) * (L // 8) - 1, 0), 0)),
            vec(RWKV_IN), vec(WIDTH), mat(128, WIDTH), vec(WIDTH), mat(128, WIDTH), mat(GATE_LORA, WIDTH),
            vec(WIDTH), vec(WIDTH), vec(WIDTH), vec(WIDTH), vec(WIDTH), mat(WIDTH, WIDTH),
        ],
        out_specs=pl.BlockSpec((L, WIDTH), lambda b, c: (b * nc + c, 0)),
        out_shape=jax.ShapeDtypeStruct((B * T, WIDTH), F32),
        scratch_shapes=[pltpu.VMEM((RWKV_HEADS, HEAD_DIM, HEAD_DIM), F32)],
        compiler_params=_params(("parallel", "arbitrary")),
        name="rwkv",
    )(proj, proj, mu, w0, wup_pad, a0, aup_pad, gup, k_k, k_a, r_k, lng, lnb, seg)


def _attn_prep_kernel(q_ref, k_ref, v_ref, qg_ref, kg_ref, seg_ref, place_ref,
                      qt_out, k_out, vt_out, bits_out, kmean_scr):
    i = pl.program_id(1)
    nb = kmean_scr.shape[0]
    BLK = MOBA_BLOCK

    @pl.when(i == 0)
    def _():
        kmean_scr[...] = jnp.zeros_like(kmean_scr)

    inv_n = 1.0 / HEAD_DIM
    q = q_ref[...]
    k = k_ref[...]
    qn = q * lax.rsqrt(_dot2(q * q, seg_ref[...]) * inv_n + RMS_EPS) * qg_ref[...]
    kn = k * lax.rsqrt(_dot2(k * k, seg_ref[...]) * inv_n + RMS_EPS) * kg_ref[...]
    qt = (qn * (HEAD_DIM ** -0.5)).T

    kmean = jnp.mean(kn, axis=0, keepdims=True)
    rown = lax.broadcasted_iota(jnp.int32, kmean_scr.shape, 0)
    kmeans = jnp.where(rown == i, kmean, kmean_scr[...])
    kmean_scr[...] = kmeans

    kaug = _dot(kn.astype(BF16), place_ref[...])
    lane = lax.broadcasted_iota(jnp.int32, kaug.shape, 1) & 127
    pos = lax.broadcasted_iota(jnp.int32, kaug.shape, 0).astype(F32)
    kaug = jnp.where(lane == HEAD_DIM, pos, kaug)
    kaug = jnp.where(lane == HEAD_DIM + 1, i.astype(F32), kaug)
    kaug = jnp.where(lane == HEAD_DIM + 2, 1.0, kaug).astype(BF16)

    vt = v_ref[...].T.astype(BF16)

    blk_id = lax.broadcasted_iota(jnp.int32, (nb, BLK), 0)
    sub = lax.broadcasted_iota(jnp.int32, (HEAD_DIM, BLK), 0)
    for h in range(ATTN_HEADS):
        sl = slice(h * HEAD_DIM, (h + 1) * HEAD_DIM)
        qth = qt[sl, :]
        slope = 2.0 ** (-8.0 * (h + 1) / ATTN_HEADS)
        far = slope * MOBA_BLOCK
        extra = jnp.where(sub == 0, slope, jnp.where(sub == 1, far, jnp.where(sub == 2, -far * i.astype(F32), 0.0)))
        qt_out[0, h, 0] = jnp.concatenate([qth, extra], axis=0).astype(BF16)
        k_out[0, h, 0] = kaug[:, h * 128:(h + 1) * 128]
        vt_out[0, h, 0] = vt[sl, :]

        gate = _dot3(kmeans[:, sl], qth)
        gate = jnp.where(blk_id < i, gate, NEG_INF)
        bits = jnp.zeros((1, BLK), jnp.int32)
        for _ in range(MOBA_TOP_K):
            mx = jnp.max(gate, axis=0, keepdims=True)
            idx = jnp.min(jnp.where(gate == mx, blk_id, nb), axis=0, keepdims=True)
            bits = bits | jnp.where(idx < i, lax.shift_left(jnp.int32(1), idx), 0)
            gate = jnp.where(blk_id == idx, -jnp.inf, gate)
        bits_out[0, h, 0] = bits


def _attn_prep(proj, B, T, qg, kg, seg, place):
    nb = T // MOBA_BLOCK
    H = ATTN_HEADS
    col = lambda c: pl.BlockSpec((MOBA_BLOCK, WIDTH), lambda b, i: (b * nb + i, c))
    const = lambda m, n: pl.BlockSpec((m, n), lambda b, i: (0, 0))
    blk = lambda r, c: pl.BlockSpec((1, H, 1, r, c), lambda b, i: (b, 0, i, 0, 0))
    return pl.pallas_call(
        _attn_prep_kernel,
        grid=(B, nb),
        in_specs=[col(ATTN_COL), col(ATTN_COL + 1), col(ATTN_COL + 2),
                  const(1, WIDTH), const(1, WIDTH), const(WIDTH, WIDTH), const(WIDTH, 2 * WIDTH)],
        out_specs=[blk(128, MOBA_BLOCK), blk(MOBA_BLOCK, 128), blk(HEAD_DIM, MOBA_BLOCK), blk(1, MOBA_BLOCK)],
        out_shape=[
            jax.ShapeDtypeStruct((B, H, nb, 128, MOBA_BLOCK), BF16),
            jax.ShapeDtypeStruct((B, H, nb, MOBA_BLOCK, 128), BF16),
            jax.ShapeDtypeStruct((B, H, nb, HEAD_DIM, MOBA_BLOCK), BF16),
            jax.ShapeDtypeStruct((B, H, nb, 1, MOBA_BLOCK), jnp.int32),
        ],
        scratch_shapes=[pltpu.VMEM((nb, WIDTH), F32)],
        compiler_params=_params(("parallel", "arbitrary")),
        name="attn_prep",
    )(proj, proj, proj, qg, kg, seg, place)


ATTN_GROUP = 4


def _attn_kernel(q_ref, bits_ref, k_ref, v_ref, o_ref, m_scr, l_scr, acc_scr):
    i = pl.program_id(2)
    BLK = MOBA_BLOCK
    G = ATTN_GROUP
    rowk = lax.broadcasted_iota(jnp.int32, (BLK, BLK), 0)
    colq = lax.broadcasted_iota(jnp.int32, (BLK, BLK), 1)
    causal = rowk <= colq

    for g in range(G):
        s = _dot(k_ref[0, g, i], q_ref[0, g, 0])
        s = jnp.where(causal, s, NEG_INF)
        m = jnp.max(s, axis=0, keepdims=True)
        p = jnp.exp(s - m)
        m_scr[g] = m
        l_scr[g] = jnp.sum(p, axis=0, keepdims=True)
        acc_scr[g] = _dot(v_ref[0, g, i], p.astype(BF16))

    def body(j, carry):
        for g in range(G):
            s = _dot(k_ref[0, g, j], q_ref[0, g, 0])
            sel = (lax.shift_right_logical(bits_ref[0, g, 0], j) & 1) == 1
            s = jnp.where(sel, s, NEG_INF)
            m_old = m_scr[g]
            m_new = jnp.maximum(m_old, jnp.max(s, axis=0, keepdims=True))
            alpha = jnp.exp(m_old - m_new)
            p = jnp.exp(s - m_new)
            m_scr[g] = m_new
            l_scr[g] = l_scr[g] * alpha + jnp.sum(p, axis=0, keepdims=True)
            acc_scr[g] = acc_scr[g] * alpha + _dot(v_ref[0, g, j], p.astype(BF16))
        return carry

    lax.fori_loop(0, i, body, 0)

    for g in range(G):
        o_ref[0, g * HEAD_DIM:(g + 1) * HEAD_DIM, :] = acc_scr[g] / l_scr[g]


def _attention(qt, kaug, vt, bits, B, T):
    nb = T // MOBA_BLOCK
    G = ATTN_GROUP
    H = ATTN_HEADS
    return pl.pallas_call(
        _attn_kernel,
        grid=(B, H // G, nb),
        in_specs=[
            pl.BlockSpec((1, G, 1, 128, MOBA_BLOCK), lambda b, hg, i: (b, hg, i, 0, 0)),
            pl.BlockSpec((1, G, 1, 1, MOBA_BLOCK), lambda b, hg, i: (b, hg, i, 0, 0)),
            pl.BlockSpec((1, G, nb, MOBA_BLOCK, 128), lambda b, hg, i: (b, hg, 0, 0, 0)),
            pl.BlockSpec((1, G, nb, HEAD_DIM, MOBA_BLOCK), lambda b, hg, i: (b, hg, 0, 0, 0)),
        ],
        out_specs=pl.BlockSpec((1, G * HEAD_DIM, MOBA_BLOCK), lambda b, hg, i: (b, hg, i)),
        out_shape=jax.ShapeDtypeStruct((B, WIDTH, T), F32),
        scratch_shapes=[pltpu.VMEM((G, 1, MOBA_BLOCK), F32), pltpu.VMEM((G, 1, MOBA_BLOCK), F32),
                        pltpu.VMEM((G, HEAD_DIM, MOBA_BLOCK), F32)],
        compiler_params=_params(("parallel", "parallel", "arbitrary")),
        name="moba_attention",
    )(qt, bits, kaug, vt)


POOL_HALO = 16


def _merge_kernel(x_ref, yr_ref, yat_ref, pp_ref, pprev_ref, g0_ref, g1_ref, g2_ref,
                  pw_ref, pscale_ref, wb_ref, wo_ref, o_ref):
    t = pl.program_id(1)
    tl = x_ref.shape[0]

    u = pp_ref[...]
    halo = jnp.where(t == 0, 0.0, pprev_ref[...])
    ext = jnp.concatenate([halo, u], axis=0)
    pos = (t * tl + lax.broadcasted_iota(jnp.int32, (tl, POOL_GROUP_DIM), 0) + 1).astype(F32)
    y_pool = []
    for gi, w in enumerate(POOL_WINDOWS):
        sl = slice(gi * POOL_GROUP_DIM, (gi + 1) * POOL_GROUP_DIM)
        acc = ext[:, sl]
        s = 1
        while s < w:
            acc = acc + pltpu.roll(acc, s, axis=0)
            s *= 2
        d = acc[POOL_HALO:] / jnp.minimum(pos, float(w)) - u[:, sl]
        y_pool.append(_dot(d.astype(BF16), pw_ref[gi]))
    y_pool = jnp.concatenate(y_pool, axis=1) * pscale_ref[...]

    br_r = _dot(yr_ref[...].astype(BF16), wb_ref[0])
    br_a = _dot(yat_ref[0].astype(BF16), wb_ref[1], TN)
    br_p = _dot(y_pool.astype(BF16), wb_ref[2])
    merged = (jax.nn.sigmoid(g0_ref[...]) * br_r + jax.nn.sigmoid(g1_ref[...]) * br_a
              + jax.nn.sigmoid(g2_ref[...]) * br_p)
    o_ref[...] = x_ref[...] + _dot(merged.astype(BF16), wo_ref[...])


def _merge(x2, y_rwkv, y_attn_t, proj, B, T, pool_w, pool_scale, w_branch, w_out, tl=512):
    tl = min(tl, T)
    nt = T // tl
    rows = lambda w, c: pl.BlockSpec((tl, w), lambda b, t: (b * nt + t, c))
    const2 = lambda m, n: pl.BlockSpec((m, n), lambda b, t: (0, 0))
    const3 = lambda a, m, n: pl.BlockSpec((a, m, n), lambda b, t: (0, 0, 0))
    return pl.pallas_call(
        _merge_kernel,
        grid=(B, nt),
        in_specs=[
            rows(D_MODEL, 0),
            rows(WIDTH, 0),
            pl.BlockSpec((1, WIDTH, tl), lambda b, t: (b, 0, t)),
            rows(WIDTH, POOL_COL),
            pl.BlockSpec((POOL_HALO, WIDTH),
                         lambda b, t: (jnp.maximum((b * nt + t) * (tl // POOL_HALO) - 1, 0), POOL_COL)),
            rows(D_MODEL, GATE_COL), rows(D_MODEL, GATE_COL + 1), rows(D_MODEL, GATE_COL + 2),
            const3(len(POOL_WINDOWS), POOL_GROUP_DIM, POOL_GROUP_DIM),
            const2(1, WIDTH),
            const3(N_BRANCH, WIDTH, D_MODEL),
            const2(D_MODEL, D_MODEL),
        ],
        out_specs=rows(D_MODEL, 0),
        out_shape=jax.ShapeDtypeStruct((B * T, D_MODEL), F32),
        compiler_params=_params(("parallel", "arbitrary")),
        name="merge",
    )(x2, y_rwkv, y_attn_t, proj, proj, proj, proj, proj, pool_w, pool_scale, w_branch, w_out)


def _ffn_kernel(x_ref, g_ref, w1_ref, w2_ref, o_ref, h_scr):
    f = pl.program_id(1)

    @pl.when(f == 0)
    def _():
        x = x_ref[...]
        ms = jnp.mean(x * x, axis=-1, keepdims=True)
        h_scr[...] = (x * lax.rsqrt(ms + RMS_EPS) * g_ref[...]).astype(BF16)
        o_ref[...] = x

    a = jnp.maximum(_dot(h_scr[...], w1_ref[...]), 0.0)
    o_ref[...] += _dot((a * a).astype(BF16), w2_ref[...])


def _ffn(x2, g, w1, w2, tm=1024, tf=1024):
    m = x2.shape[0]
    tm = min(tm, m)
    return pl.pallas_call(
        _ffn_kernel,
        grid=(m // tm, D_FF // tf),
        in_specs=[
            pl.BlockSpec((tm, D_MODEL), lambda i, f: (i, 0)),
            pl.BlockSpec((1, D_MODEL), lambda i, f: (0, 0)),
            pl.BlockSpec((D_MODEL, tf), lambda i, f: (0, f)),
            pl.BlockSpec((tf, D_MODEL), lambda i, f: (f, 0)),
        ],
        out_specs=pl.BlockSpec((tm, D_MODEL), lambda i, f: (i, 0)),
        out_shape=jax.ShapeDtypeStruct((m, D_MODEL), F32),
        scratch_shapes=[pltpu.VMEM((tm, D_MODEL), BF16)],
        compiler_params=_params(("parallel", "arbitrary")),
        name="ffn",
    )(x2, g, w1, w2)


def _segment_ones():
    idx = np.arange(WIDTH) // HEAD_DIM
    return jnp.asarray(idx[:, None] == idx[None, :], BF16)


def _head_placement():
    src = np.arange(WIDTH)
    dst = (src // HEAD_DIM) * 128 + src % HEAD_DIM
    m = np.zeros((WIDTH, 2 * WIDTH), np.float32)
    m[src, dst] = 1.0
    return jnp.asarray(m, BF16)


def kernel(x, norm1_g, w_in, shift_mu, w0, w_up, a0, a_up, g_up, k_k, k_a, r_k, lnx_g, lnx_b,
           qn_g, kn_g, pool_w, pool_scale, w_branch, w_out, norm2_g, w_ff1, w_ff2):
    B, T, D = x.shape
    depth = w_in.shape[0]
    assert D == D_MODEL and T % MOBA_BLOCK == 0
    out_dtype = x.dtype
    x2 = x.astype(F32).reshape(B * T, D)
    seg = _segment_ones()
    place = _head_placement()
    row = lambda a: a.reshape(1, -1).astype(F32)
    zeros_lora = jnp.zeros((64, WIDTH), F32)

    for l in range(depth):
        w = w_in[l]
        w_pad = jnp.concatenate(
            [w[:, :RWKV_IN], jnp.zeros((D, RWKV_PAD - RWKV_IN), w.dtype), w[:, RWKV_IN:]], axis=1).astype(BF16)
        proj = _inproj(x2, row(norm1_g[l]), w_pad)

        wup_pad = jnp.concatenate([w_up[l], zeros_lora], axis=0)
        aup_pad = jnp.concatenate([zeros_lora, a_up[l]], axis=0)
        y_rwkv = _rwkv(proj, B, T, row(shift_mu[l]), row(w0[l]), wup_pad, row(a0[l]), aup_pad, g_up[l],
                       row(k_k[l]), row(k_a[l]), row(r_k[l]), row(lnx_g[l]), row(lnx_b[l]), seg)

        qt, kaug, vt, bits = _attn_prep(proj, B, T, row(jnp.tile(qn_g[l], ATTN_HEADS)),
                                        row(jnp.tile(kn_g[l], ATTN_HEADS)), seg, place)
        y_attn_t = _attention(qt, kaug, vt, bits, B, T)

        x2 = _merge(x2, y_rwkv, y_attn_t, proj, B, T, pool_w[l].astype(BF16), row(pool_scale[l]),
                    w_branch[l].astype(BF16), w_out[l].astype(BF16))
        x2 = _ffn(x2, row(norm2_g[l]), w_ff1[l].astype(BF16), w_ff2[l].astype(BF16))

    return x2.reshape(B, T, D).astype(out_dtype)
```

```python
import numpy as np
import jax
import jax.numpy as jnp
from jax import lax
from jax.experimental import pallas as pl
from jax.experimental.pallas import tpu as pltpu

F32 = jnp.float32
BF16 = jnp.bfloat16

D_MODEL = 1024
RWKV_HEADS = 8
HEAD_DIM = 64
WIDTH = RWKV_HEADS * HEAD_DIM
DECAY_LORA = 64
ICLR_LORA = 64
GATE_LORA = 128
ATTN_HEADS = 8
MOBA_BLOCK = 256
MOBA_TOP_K = 3
POOL_WINDOWS = (2, 4, 8, 16)
POOL_GROUP_DIM = 128
N_BRANCH = 3
D_FF = 4 * D_MODEL
RMS_EPS = 1e-6
GN_EPS = 64e-5
NEG_INF = -1e30

RWKV_IN = 3 * WIDTH + DECAY_LORA + ICLR_LORA + GATE_LORA
RWKV_PAD = 2048
ATTN_COL = RWKV_PAD // WIDTH
POOL_COL = ATTN_COL + 3
GATE_COL = (RWKV_PAD + 4 * WIDTH) // D_MODEL
N_PROJ = RWKV_PAD + 4 * WIDTH + N_BRANCH * D_MODEL

CHUNK = 64
RWKV_SEQS = 2
VMEM_LIMIT = 56 * 1024 * 1024

NN = (((1,), (0,)), ((), ()))
NT = (((1,), (1,)), ((), ()))
TN = (((0,), (0,)), ((), ()))


def _dot(a, b, dims=NN):
    return lax.dot_general(a, b, dims, preferred_element_type=F32)


def _dot1(a, b, dims=NN):
    return _dot(a.astype(BF16), b.astype(BF16), dims)


def _split(a):
    hi = a.astype(BF16)
    lo = (a - hi.astype(F32)).astype(BF16)
    return hi, lo


def _dot3(a, b, dims=NN):
    ah, al = _split(a)
    bh, bl = _split(b)
    return _dot(ah, bh, dims) + (_dot(ah, bl, dims) + _dot(al, bh, dims))


def _dot2(a, b_bf16, dims=NN):
    ah, al = _split(a)
    return _dot(ah, b_bf16, dims) + _dot(al, b_bf16, dims)


def _params(sem):
    return pltpu.CompilerParams(dimension_semantics=sem, vmem_limit_bytes=VMEM_LIMIT)


def _inproj_kernel(x_ref, g_ref, w_ref, o_ref, h_scr):
    @pl.when(pl.program_id(1) == 0)
    def _():
        x = x_ref[...]
        ms = jnp.mean(x * x, axis=-1, keepdims=True)
        h_scr[...] = (x * lax.rsqrt(ms + RMS_EPS) * g_ref[...]).astype(BF16)

    o_ref[...] = _dot(h_scr[...], w_ref[...])


def _inproj(x2, g, w_pad, tm=1024, tn=1024):
    m = x2.shape[0]
    tm = min(tm, m)
    return pl.pallas_call(
        _inproj_kernel,
        grid=(m // tm, N_PROJ // tn),
        in_specs=[
            pl.BlockSpec((tm, D_MODEL), lambda i, j: (i, 0)),
            pl.BlockSpec((1, D_MODEL), lambda i, j: (0, 0)),
            pl.BlockSpec((D_MODEL, tn), lambda i, j: (0, j)),
        ],
        out_specs=pl.BlockSpec((tm, tn), lambda i, j: (i, j)),
        out_shape=jax.ShapeDtypeStruct((m, N_PROJ), F32),
        scratch_shapes=[pltpu.VMEM((tm, D_MODEL), BF16)],
        compiler_params=_params(("parallel", "arbitrary")),
        name="inproj",
    )(x2, g, w_pad)


def _rwkv_kernel(p_ref, prev_ref, mu_ref, w0_ref, wup_ref, a0_ref, aup_ref, gup_ref,
                 kk_ref, ka_ref, rk_ref, lng_ref, lnb_ref, seg_ref, y_ref, s_scr):
    c = pl.program_id(1)
    L = CHUNK
    W = WIDTH
    nseq = p_ref.shape[0]
    R = nseq * L

    @pl.when(c == 0)
    def _():
        s_scr[...] = jnp.zeros_like(s_scr)

    p = p_ref[...].reshape(R, RWKV_IN)
    row_p = lax.broadcasted_iota(jnp.int32, p.shape, 0)
    shifted = pltpu.roll(p, 1, axis=0)
    for b in range(nseq):
        first = jnp.where(c == 0, 0.0, prev_ref[b, 7:8, :])
        shifted = jnp.where(row_p == b * L, first, shifted)
    ps = p + (shifted - p) * mu_ref[...]

    r = ps[:, 0:W]
    k = ps[:, W:2 * W]
    v = ps[:, 2 * W:3 * W]
    lo = ps[:, 3 * W:3 * W + 128]
    xg = ps[:, 3 * W + 128:3 * W + 256]

    wpre = w0_ref[...] + _dot3(jnp.tanh(lo), wup_ref[...])
    logw = -float(np.exp(-0.5)) * jax.nn.sigmoid(wpre)
    a = jax.nn.sigmoid(a0_ref[...] + _dot3(lo, aup_ref[...]))
    g = _dot3(jax.nn.sigmoid(xg), gup_ref[...])

    kk = k * kk_ref[...]
    ss = _dot2(kk * kk, seg_ref[...])
    kk = kk * lax.rsqrt(jnp.maximum(ss, 1e-24))
    k2 = k * (1.0 + (a - 1.0) * ka_ref[...])
    avec = -kk
    bvec = kk * a

    row = lax.broadcasted_iota(jnp.int32, (R, W), 0)
    tloc = row & (L - 1)
    cl = logw
    s = 1
    while s < L:
        cl = cl + jnp.where(tloc >= s, pltpu.roll(cl, s, axis=0), 0.0)
        s *= 2
    cl_last = [cl[(b + 1) * L - 1:(b + 1) * L, :] for b in range(nseq)]
    cl_end = cl_last[nseq - 1]
    for b in range(nseq - 2, -1, -1):
        cl_end = jnp.where(row < (b + 1) * L, cl_last[b], cl_end)
    rt = r * jnp.exp(cl)
    at = avec * jnp.exp(cl - logw)
    e_neg = jnp.exp(-cl)
    kt = k2 * e_neg
    bt = bvec * e_neg
    e_rem = jnp.exp(cl_end - cl)
    kh = k2 * e_rem
    bh = bvec * e_rem
    p_last = [jnp.exp(cl_last[b]) for b in range(nseq)]
    rkk = r * k2 * rk_ref[...]

    row64 = lax.broadcasted_iota(jnp.int32, (L, L), 0)
    col64 = lax.broadcasted_iota(jnp.int32, (L, L), 1)
    strict = row64 > col64
    incl = row64 >= col64

    items = [(b, h) for b in range(nseq) for h in range(RWKV_HEADS)]
    n = range(len(items))
    rs = [slice(b * L, (b + 1) * L) for b, _ in items]
    hs = [slice(h * HEAD_DIM, (h + 1) * HEAD_DIM) for _, h in items]
    each = lambda f: [f(i) for i in n]
    mm = _dot1

    s0 = each(lambda i: s_scr[i])
    vh = each(lambda i: v[rs[i], hs[i]])
    ar = each(lambda i: jnp.concatenate([at[rs[i], hs[i]], rt[rs[i], hs[i]]], axis=0))
    mk = each(lambda i: mm(ar[i], kt[rs[i], hs[i]], NT))
    mb = each(lambda i: mm(ar[i], bt[rs[i], hs[i]], NT))
    m_ab = each(lambda i: jnp.where(strict, mb[i][:L], 0.0))
    m_ak = each(lambda i: jnp.where(strict, mk[i][:L], 0.0))
    m_rk = each(lambda i: jnp.where(incl, mk[i][L:], 0.0))
    m_rb = each(lambda i: jnp.where(incl, mb[i][L:], 0.0))

    eye = (row64 == col64).astype(F32)
    blk16 = (row64 >> 4) == (col64 >> 4)
    blk32 = (row64 >> 5) == (col64 >> 5)
    d1 = each(lambda i: jnp.where(blk16, m_ab[i], 0.0))
    e1 = each(lambda i: jnp.where(blk32 & jnp.logical_not(blk16), m_ab[i], 0.0))
    e2 = each(lambda i: jnp.where(jnp.logical_not(blk32), m_ab[i], 0.0))
    x = each(lambda i: eye + d1[i])
    d2 = each(lambda i: mm(d1[i], d1[i]))
    ars = each(lambda i: mm(ar[i], s0[i], NT))
    akv = each(lambda i: mm(m_ak[i], vh[i]))
    x = each(lambda i: x[i] + mm(d2[i], x[i]))
    d4 = each(lambda i: mm(d2[i], d2[i]))
    rkv = each(lambda i: mm(m_rk[i], vh[i]))
    x = each(lambda i: x[i] + mm(d4[i], x[i]))
    d8 = each(lambda i: mm(d4[i], d4[i]))
    svk = each(lambda i: mm(vh[i], kh[rs[i], hs[i]], TN))
    x = each(lambda i: x[i] + mm(d8[i], x[i]))
    t = each(lambda i: mm(e1[i], x[i]))
    x = each(lambda i: x[i] + mm(x[i], t[i]))
    t = each(lambda i: mm(e2[i], x[i]))
    x = each(lambda i: x[i] + mm(x[i], t[i]))
    u = each(lambda i: mm(x[i], ars[i][:L] + akv[i]))
    o = each(lambda i: ars[i][L:] + rkv[i] + mm(m_rb[i], u[i]))
    sub = each(lambda i: mm(u[i], bh[rs[i], hs[i]], TN))
    for i in n:
        s_scr[i] = s0[i] * p_last[items[i][0]][:, hs[i]] + svk[i] + sub[i]

    for i in n:
        mean = jnp.mean(o[i], axis=-1, keepdims=True)
        var = jnp.mean(jnp.square(o[i] - mean), axis=-1, keepdims=True)
        on = (o[i] - mean) * lax.rsqrt(var + GN_EPS) * lng_ref[:, hs[i]] + lnb_ref[:, hs[i]]
        bonus = jnp.sum(rkk[rs[i], hs[i]], axis=-1, keepdims=True) * vh[i]
        y_ref[items[i][0], :, hs[i]] = (on + bonus) * g[rs[i], hs[i]]


def _rwkv(proj, B, T, mu, w0, wup_pad, a0, aup_pad, gup, k_k, k_a, r_k, lng, lnb, seg):
    L = CHUNK
    nc = T // L
    nseq = RWKV_SEQS if B % RWKV_SEQS == 0 else 1
    proj3 = proj.reshape(B, T, N_PROJ)
    vec = lambda n: pl.BlockSpec((1, n), lambda b, c: (0, 0))
    mat = lambda m, n: pl.BlockSpec((m, n), lambda b, c: (0, 0))
    y = pl.pallas_call(
        _rwkv_kernel,
        grid=(B // nseq, nc),
        in_specs=[
            pl.BlockSpec((nseq, L, RWKV_IN), lambda b, c: (b, c, 0)),
            pl.BlockSpec((nseq, 8, RWKV_IN), lambda b, c: (b, jnp.maximum(c * (L // 8) - 1, 0), 0)),
            vec(RWKV_IN), vec(WIDTH), mat(128, WIDTH), vec(WIDTH), mat(128, WIDTH), mat(GATE_LORA, WIDTH),
            vec(WIDTH), vec(WIDTH), vec(WIDTH), vec(WIDTH), vec(WIDTH), mat(WIDTH, WIDTH),
        ],
        out_specs=pl.BlockSpec((nseq, L, WIDTH), lambda b, c: (b, c, 0)),
        out_shape=jax.ShapeDtypeStruct((B, T, WIDTH), F32),
        scratch_shapes=[pltpu.VMEM((nseq * RWKV_HEADS, HEAD_DIM, HEAD_DIM), F32)],
        compiler_params=_params(("parallel", "arbitrary")),
        name="rwkv",
    )(proj3, proj3, mu, w0, wup_pad, a0, aup_pad, gup, k_k, k_a, r_k, lng, lnb, seg)
    return y.reshape(B * T, WIDTH)


def _bf16_split3(c):
    out, r = [], float(c)
    for _ in range(3):
        h = float(np.asarray(r, np.float32).astype(BF16).astype(np.float32))
        out.append(h)
        r -= h
    return out


LOG2E = float(np.log2(np.e))
Q_ROWS = 128
BIAS_ROW = HEAD_DIM
PEN_ROW = HEAD_DIM + 8
V_ROWS = HEAD_DIM + 16


def _attn_prep_kernel(q_ref, k_ref, v_ref, qg_ref, kg_ref, seg_ref, place_ref,
                      qt_out, k_out, vt_out, kmean_scr):
    i = pl.program_id(1)
    nbp = kmean_scr.shape[0]
    BLK = MOBA_BLOCK

    @pl.when(i == 0)
    def _():
        kmean_scr[...] = jnp.zeros_like(kmean_scr)

    inv_n = 1.0 / HEAD_DIM
    q = q_ref[...]
    k = k_ref[...]
    qn = q * lax.rsqrt(_dot2(q * q, seg_ref[...]) * inv_n + RMS_EPS) * qg_ref[...]
    kn = k * lax.rsqrt(_dot2(k * k, seg_ref[...]) * inv_n + RMS_EPS) * kg_ref[...]
    qt = (qn * (LOG2E * HEAD_DIM ** -0.5)).T

    kmean = jnp.mean(kn, axis=0, keepdims=True)
    rown = lax.broadcasted_iota(jnp.int32, kmean_scr.shape, 0)
    kmeans = jnp.where(rown == i, kmean, kmean_scr[...])
    kmean_scr[...] = kmeans

    kaug = _dot(kn.astype(BF16), place_ref[...])
    lane = lax.broadcasted_iota(jnp.int32, kaug.shape, 1) & 127
    pos = lax.broadcasted_iota(jnp.int32, kaug.shape, 0).astype(F32)
    kaug = jnp.where((lane >= BIAS_ROW) & (lane < BIAS_ROW + 3), pos, kaug)
    kaug = jnp.where((lane >= BIAS_ROW + 3) & (lane < BIAS_ROW + 6), i.astype(F32), kaug)
    kaug = jnp.where(lane == PEN_ROW + i, 1.0, kaug).astype(BF16)

    vt = v_ref[...].T
    sub16 = lax.broadcasted_iota(jnp.int32, (V_ROWS - HEAD_DIM, BLK), 0)
    ones_rows = jnp.where(sub16 == 0, 1.0, 0.0)

    blk_id = lax.broadcasted_iota(jnp.int32, (nbp, BLK), 0)
    sub8 = lax.broadcasted_iota(jnp.int32, (8, BLK), 0)
    tail = jnp.zeros((Q_ROWS - PEN_ROW - nbp, BLK), F32)
    for h in range(ATTN_HEADS):
        sl = slice(h * HEAD_DIM, (h + 1) * HEAD_DIM)
        qth = qt[sl, :]

        gate = _dot3(kmeans[:, sl], qth)
        gate = jnp.where(blk_id < i, gate, NEG_INF)
        chosen = blk_id == i
        for _ in range(MOBA_TOP_K):
            mx = jnp.max(gate, axis=0, keepdims=True)
            idx = jnp.min(jnp.where(gate == mx, blk_id, nbp), axis=0, keepdims=True)
            hit = blk_id == idx
            chosen = chosen | (hit & (blk_id < i))
            gate = jnp.where(hit, -jnp.inf, gate)
        penalty = jnp.where(chosen, 0.0, NEG_INF)

        slope = 2.0 ** (-8.0 * (h + 1) / ATTN_HEADS)
        cvals = _bf16_split3(slope * LOG2E) + _bf16_split3(slope * LOG2E * MOBA_BLOCK)
        consts = jnp.zeros((8, BLK), F32)
        for r, cv in enumerate(cvals):
            consts = jnp.where(sub8 == r, cv, consts)
        qt_out[0, h, 0] = jnp.concatenate([qth, consts, penalty, tail], axis=0).astype(BF16)
        k_out[0, h, 0] = kaug[:, h * 128:(h + 1) * 128]
        vt_out[0, h, 0] = jnp.concatenate([vt[sl, :], ones_rows], axis=0).astype(BF16)


def _attn_prep(proj, B, T, qg, kg, seg, place):
    nb = T // MOBA_BLOCK
    nbp = -(-nb // 8) * 8
    assert PEN_ROW + nbp <= Q_ROWS
    H = ATTN_HEADS
    col = lambda c: pl.BlockSpec((MOBA_BLOCK, WIDTH), lambda b, i: (b * nb + i, c))
    const = lambda m, n: pl.BlockSpec((m, n), lambda b, i: (0, 0))
    blk = lambda r, c: pl.BlockSpec((1, H, 1, r, c), lambda b, i: (b, 0, i, 0, 0))
    return pl.pallas_call(
        _attn_prep_kernel,
        grid=(B, nb),
        in_specs=[col(ATTN_COL), col(ATTN_COL + 1), col(ATTN_COL + 2),
                  const(1, WIDTH), const(1, WIDTH), const(WIDTH, WIDTH), const(WIDTH, 2 * WIDTH)],
        out_specs=[blk(Q_ROWS, MOBA_BLOCK), blk(MOBA_BLOCK, 128), blk(V_ROWS, MOBA_BLOCK)],
        out_shape=[
            jax.ShapeDtypeStruct((B, H, nb, Q_ROWS, MOBA_BLOCK), BF16),
            jax.ShapeDtypeStruct((B, H, nb, MOBA_BLOCK, 128), BF16),
            jax.ShapeDtypeStruct((B, H, nb, V_ROWS, MOBA_BLOCK), BF16),
        ],
        scratch_shapes=[pltpu.VMEM((nbp, WIDTH), F32)],
        compiler_params=_params(("parallel", "arbitrary")),
        name="attn_prep",
    )(proj, proj, proj, qg, kg, seg, place)


def _attn_kernel(q_ref, k_ref, v_ref, o_ref):
    i = pl.program_id(1)
    BLK = MOBA_BLOCK
    rowk = lax.broadcasted_iota(jnp.int32, (BLK, BLK), 0)
    colq = lax.broadcasted_iota(jnp.int32, (BLK, BLK), 1)
    causal = rowk <= colq

    heads = range(ATTN_HEADS)
    each = lambda f: [f(h) for h in heads]
    q = each(lambda h: q_ref[0, h, 0])

    s = each(lambda h: jnp.where(causal, _dot(k_ref[0, h, i], q[h]), NEG_INF))
    m0 = each(lambda h: jnp.max(s[h], axis=0, keepdims=True))
    p = each(lambda h: jnp.exp2(s[h] - m0[h]))
    acc0 = each(lambda h: _dot(v_ref[0, h, i], p[h].astype(BF16)))

    def past_block(j, carry):
        m_old, acc_old = carry
        s = each(lambda h: _dot(k_ref[0, h, j], q[h]))
        m_new = each(lambda h: jnp.maximum(m_old[h], jnp.max(s[h], axis=0, keepdims=True)))
        alpha = each(lambda h: jnp.exp2(m_old[h] - m_new[h]))
        p = each(lambda h: jnp.exp2(s[h] - m_new[h]).astype(BF16))
        pv = each(lambda h: _dot(v_ref[0, h, j], p[h]))
        acc_new = each(lambda h: acc_old[h] * alpha[h] + pv[h])
        return m_new, acc_new

    _, acc = lax.fori_loop(0, i, past_block, (m0, acc0))

    for h in heads:
        o_ref[0, h * HEAD_DIM:(h + 1) * HEAD_DIM, :] = acc[h][:HEAD_DIM] / acc[h][HEAD_DIM:HEAD_DIM + 1]


def _attention(qt, kaug, vt, B, T):
    nb = T // MOBA_BLOCK
    H = ATTN_HEADS
    whole = lambda r, c: pl.BlockSpec((1, H, nb, r, c), lambda b, i: (b, 0, 0, 0, 0),
                                      pipeline_mode=pl.Buffered(1))
    return pl.pallas_call(
        _attn_kernel,
        grid=(B, nb),
        in_specs=[
            pl.BlockSpec((1, H, 1, Q_ROWS, MOBA_BLOCK), lambda b, i: (b, 0, i, 0, 0)),
            whole(MOBA_BLOCK, 128),
            whole(V_ROWS, MOBA_BLOCK),
        ],
        out_specs=pl.BlockSpec((1, WIDTH, MOBA_BLOCK), lambda b, i: (b, 0, i)),
        out_shape=jax.ShapeDtypeStruct((B, WIDTH, T), F32),
        compiler_params=_params(("parallel", "arbitrary")),
        name="moba_attention",
    )(qt, kaug, vt)


POOL_HALO = 16


def _merge_kernel(x_ref, yr_ref, yat_ref, pp_ref, pprev_ref, g0_ref, g1_ref, g2_ref,
                  pw_ref, pscale_ref, wb_ref, wo_ref, o_ref):
    t = pl.program_id(1)
    tl = x_ref.shape[0]

    u = pp_ref[...]
    halo = jnp.where(t == 0, 0.0, pprev_ref[...])
    ext = jnp.concatenate([halo, u], axis=0)
    pos = (t * tl + lax.broadcasted_iota(jnp.int32, (tl, POOL_GROUP_DIM), 0) + 1).astype(F32)
    y_pool = []
    for gi, w in enumerate(POOL_WINDOWS):
        sl = slice(gi * POOL_GROUP_DIM, (gi + 1) * POOL_GROUP_DIM)
        acc = ext[:, sl]
        s = 1
        while s < w:
            acc = acc + pltpu.roll(acc, s, axis=0)
            s *= 2
        d = acc[POOL_HALO:] / jnp.minimum(pos, float(w)) - u[:, sl]
        y_pool.append(_dot(d.astype(BF16), pw_ref[gi]))
    y_pool = jnp.concatenate(y_pool, axis=1) * pscale_ref[...]

    br_r = _dot(yr_ref[...].astype(BF16), wb_ref[0])
    br_a = _dot(yat_ref[0].astype(BF16), wb_ref[1], TN)
    br_p = _dot(y_pool.astype(BF16), wb_ref[2])
    merged = (jax.nn.sigmoid(g0_ref[...]) * br_r + jax.nn.sigmoid(g1_ref[...]) * br_a
              + jax.nn.sigmoid(g2_ref[...]) * br_p)
    o_ref[...] = x_ref[...] + _dot(merged.astype(BF16), wo_ref[...])


def _merge(x2, y_rwkv, y_attn_t, proj, B, T, pool_w, pool_scale, w_branch, w_out, tl=512):
    tl = min(tl, T)
    nt = T // tl
    rows = lambda w, c: pl.BlockSpec((tl, w), lambda b, t: (b * nt + t, c))
    const2 = lambda m, n: pl.BlockSpec((m, n), lambda b, t: (0, 0))
    const3 = lambda a, m, n: pl.BlockSpec((a, m, n), lambda b, t: (0, 0, 0))
    return pl.pallas_call(
        _merge_kernel,
        grid=(B, nt),
        in_specs=[
            rows(D_MODEL, 0),
            rows(WIDTH, 0),
            pl.BlockSpec((1, WIDTH, tl), lambda b, t: (b, 0, t)),
            rows(WIDTH, POOL_COL),
            pl.BlockSpec((POOL_HALO, WIDTH),
                         lambda b, t: (jnp.maximum((b * nt + t) * (tl // POOL_HALO) - 1, 0), POOL_COL)),
            rows(D_MODEL, GATE_COL), rows(D_MODEL, GATE_COL + 1), rows(D_MODEL, GATE_COL + 2),
            const3(len(POOL_WINDOWS), POOL_GROUP_DIM, POOL_GROUP_DIM),
            const2(1, WIDTH),
            const3(N_BRANCH, WIDTH, D_MODEL),
            const2(D_MODEL, D_MODEL),
        ],
        out_specs=rows(D_MODEL, 0),
        out_shape=jax.ShapeDtypeStruct((B * T, D_MODEL), F32),
        compiler_params=_params(("parallel", "arbitrary")),
        name="merge",
    )(x2, y_rwkv, y_attn_t, proj, proj, proj, proj, proj, pool_w, pool_scale, w_branch, w_out)


def _ffn_kernel(x_ref, g_ref, w1_ref, w2_ref, o_ref, h_scr):
    f = pl.program_id(1)

    @pl.when(f == 0)
    def _():
        x = x_ref[...]
        ms = jnp.mean(x * x, axis=-1, keepdims=True)
        h_scr[...] = (x * lax.rsqrt(ms + RMS_EPS) * g_ref[...]).astype(BF16)
        o_ref[...] = x

    a = jnp.maximum(_dot(h_scr[...], w1_ref[...]), 0.0)
    o_ref[...] += _dot((a * a).astype(BF16), w2_ref[...])


def _ffn(x2, g, w1, w2, tm=1024, tf=1024):
    m = x2.shape[0]
    tm = min(tm, m)
    return pl.pallas_call(
        _ffn_kernel,
        grid=(m // tm, D_FF // tf),
        in_specs=[
            pl.BlockSpec((tm, D_MODEL), lambda i, f: (i, 0)),
            pl.BlockSpec((1, D_MODEL), lambda i, f: (0, 0)),
            pl.BlockSpec((D_MODEL, tf), lambda i, f: (0, f)),
            pl.BlockSpec((tf, D_MODEL), lambda i, f: (f, 0)),
        ],
        out_specs=pl.BlockSpec((tm, D_MODEL), lambda i, f: (i, 0)),
        out_shape=jax.ShapeDtypeStruct((m, D_MODEL), F32),
        scratch_shapes=[pltpu.VMEM((tm, D_MODEL), BF16)],
        compiler_params=_params(("parallel", "arbitrary")),
        name="ffn",
    )(x2, g, w1, w2)


def _segment_ones():
    idx = np.arange(WIDTH) // HEAD_DIM
    return jnp.asarray(idx[:, None] == idx[None, :], BF16)


def _head_placement():
    src = np.arange(WIDTH)
    dst = (src // HEAD_DIM) * 128 + src % HEAD_DIM
    m = np.zeros((WIDTH, 2 * WIDTH), np.float32)
    m[src, dst] = 1.0
    return jnp.asarray(m, BF16)


def kernel(x, norm1_g, w_in, shift_mu, w0, w_up, a0, a_up, g_up, k_k, k_a, r_k, lnx_g, lnx_b,
           qn_g, kn_g, pool_w, pool_scale, w_branch, w_out, norm2_g, w_ff1, w_ff2):
    B, T, D = x.shape
    depth = w_in.shape[0]
    assert D == D_MODEL and T % MOBA_BLOCK == 0
    out_dtype = x.dtype
    x2 = x.astype(F32).reshape(B * T, D)
    seg = _segment_ones()
    place = _head_placement()
    row = lambda a: a.reshape(1, -1).astype(F32)
    zeros_lora = jnp.zeros((64, WIDTH), F32)

    for l in range(depth):
        w = w_in[l]
        w_pad = jnp.concatenate(
            [w[:, :RWKV_IN], jnp.zeros((D, RWKV_PAD - RWKV_IN), w.dtype), w[:, RWKV_IN:]], axis=1).astype(BF16)
        proj = _inproj(x2, row(norm1_g[l]), w_pad)

        wup_pad = jnp.concatenate([w_up[l], zeros_lora], axis=0)
        aup_pad = jnp.concatenate([zeros_lora, a_up[l]], axis=0)
        y_rwkv = _rwkv(proj, B, T, row(shift_mu[l]), row(w0[l]), wup_pad, row(a0[l]), aup_pad, g_up[l],
                       row(k_k[l]), row(k_a[l]), row(r_k[l]), row(lnx_g[l]), row(lnx_b[l]), seg)

        qt, kaug, vt = _attn_prep(proj, B, T, row(jnp.tile(qn_g[l], ATTN_HEADS)),
                                  row(jnp.tile(kn_g[l], ATTN_HEADS)), seg, place)
        y_attn_t = _attention(qt, kaug, vt, B, T)

        x2 = _merge(x2, y_rwkv, y_attn_t, proj, B, T, pool_w[l].astype(BF16), row(pool_scale[l]),
                    w_branch[l].astype(BF16), w_out[l].astype(BF16))
        x2 = _ffn(x2, row(norm2_g[l]), w_ff1[l].astype(BF16), w_ff2[l].astype(BF16))

    return x2.reshape(B, T, D).astype(out_dtype)
```

```python
import numpy as np
import jax
import jax.numpy as jnp
from jax import lax
from jax.experimental import pallas as pl
from jax.experimental.pallas import tpu as pltpu

F32 = jnp.float32
BF16 = jnp.bfloat16

D_MODEL = 1024
RWKV_HEADS = 8
HEAD_DIM = 64
WIDTH = RWKV_HEADS * HEAD_DIM
DECAY_LORA = 64
ICLR_LORA = 64
GATE_LORA = 128
ATTN_HEADS = 8
MOBA_BLOCK = 256
MOBA_TOP_K = 3
POOL_WINDOWS = (2, 4, 8, 16)
POOL_GROUP_DIM = 128
N_BRANCH = 3
D_FF = 4 * D_MODEL
RMS_EPS = 1e-6
GN_EPS = 64e-5
NEG_INF = -1e30

RWKV_IN = 3 * WIDTH + DECAY_LORA + ICLR_LORA + GATE_LORA
RWKV_PAD = 2048
ATTN_COL = RWKV_PAD // WIDTH
POOL_COL = ATTN_COL + 3
N_MIX = RWKV_PAD + 4 * WIDTH
N_GATE = N_BRANCH * D_MODEL

CHUNK = 64
RWKV_SEQS = 2
VMEM_LIMIT = 56 * 1024 * 1024

NN = (((1,), (0,)), ((), ()))
NT = (((1,), (1,)), ((), ()))
TN = (((0,), (0,)), ((), ()))


def _dot(a, b, dims=NN):
    return lax.dot_general(a, b, dims, preferred_element_type=F32)


def _dot1(a, b, dims=NN):
    return _dot(a.astype(BF16), b.astype(BF16), dims)


def _split(a):
    hi = a.astype(BF16)
    lo = (a - hi.astype(F32)).astype(BF16)
    return hi, lo


def _dot3(a, b, dims=NN):
    ah, al = _split(a)
    bh, bl = _split(b)
    return _dot(ah, bh, dims) + (_dot(ah, bl, dims) + _dot(al, bh, dims))


def _dot2(a, b_bf16, dims=NN):
    ah, al = _split(a)
    return _dot(ah, b_bf16, dims) + _dot(al, b_bf16, dims)


def _params(sem):
    return pltpu.CompilerParams(dimension_semantics=sem, vmem_limit_bytes=VMEM_LIMIT)


def _inproj_kernel(x_ref, g_ref, wm_ref, wg_ref, om_ref, og_ref):
    x = x_ref[...]
    ms = jnp.mean(x * x, axis=-1, keepdims=True)
    h = (x * lax.rsqrt(ms + RMS_EPS) * g_ref[...]).astype(BF16)
    tn = D_MODEL
    for n in range(N_MIX // tn):
        om_ref[:, n * tn:(n + 1) * tn] = _dot(h, wm_ref[:, n * tn:(n + 1) * tn])
    for n in range(N_GATE // tn):
        og_ref[:, n * tn:(n + 1) * tn] = _dot(h, wg_ref[:, n * tn:(n + 1) * tn]).astype(BF16)


def _inproj(x2, g, w_mix, w_gate, tm=512):
    m = x2.shape[0]
    tm = min(tm, m)
    resident = lambda n: pl.BlockSpec((D_MODEL, n), lambda i: (0, 0), pipeline_mode=pl.Buffered(1))
    return pl.pallas_call(
        _inproj_kernel,
        grid=(m // tm,),
        in_specs=[
            pl.BlockSpec((tm, D_MODEL), lambda i: (i, 0)),
            pl.BlockSpec((1, D_MODEL), lambda i: (0, 0)),
            resident(N_MIX),
            resident(N_GATE),
        ],
        out_specs=[pl.BlockSpec((tm, N_MIX), lambda i: (i, 0)), pl.BlockSpec((tm, N_GATE), lambda i: (i, 0))],
        out_shape=[jax.ShapeDtypeStruct((m, N_MIX), F32), jax.ShapeDtypeStruct((m, N_GATE), BF16)],
        compiler_params=_params(("parallel",)),
        name="inproj",
    )(x2, g, w_mix, w_gate)


def _rwkv_kernel(p_ref, prev_ref, mu_ref, w0_ref, wup_ref, a0_ref, aup_ref, gup_ref,
                 kk_ref, ka_ref, rk_ref, lng_ref, lnb_ref, seg_ref, y_ref, s_scr):
    c = pl.program_id(1)
    L = CHUNK
    W = WIDTH
    nseq = p_ref.shape[0]
    R = nseq * L

    @pl.when(c == 0)
    def _():
        s_scr[...] = jnp.zeros_like(s_scr)

    p = p_ref[...].reshape(R, RWKV_IN)
    row_p = lax.broadcasted_iota(jnp.int32, p.shape, 0)
    shifted = pltpu.roll(p, 1, axis=0)
    for b in range(nseq):
        first = jnp.where(c == 0, 0.0, prev_ref[b, 7:8, :])
        shifted = jnp.where(row_p == b * L, first, shifted)
    ps = p + (shifted - p) * mu_ref[...]

    r = ps[:, 0:W]
    k = ps[:, W:2 * W]
    v = ps[:, 2 * W:3 * W]
    lo = ps[:, 3 * W:3 * W + 128]
    xg = ps[:, 3 * W + 128:3 * W + 256]

    wpre = w0_ref[...] + _dot3(jnp.tanh(lo), wup_ref[...])
    logw = -float(np.exp(-0.5)) * jax.nn.sigmoid(wpre)
    a = jax.nn.sigmoid(a0_ref[...] + _dot1(lo, aup_ref[...]))
    g = _dot1(jax.nn.sigmoid(xg), gup_ref[...])

    kk = k * kk_ref[...]
    ss = _dot2(kk * kk, seg_ref[...])
    kk = kk * lax.rsqrt(jnp.maximum(ss, 1e-24))
    k2 = k * (1.0 + (a - 1.0) * ka_ref[...])
    avec = -kk
    bvec = kk * a

    row = lax.broadcasted_iota(jnp.int32, (R, W), 0)
    tloc = row & (L - 1)
    cl = logw
    s = 1
    while s < L:
        cl = cl + jnp.where(tloc >= s, pltpu.roll(cl, s, axis=0), 0.0)
        s *= 2
    cl_last = [cl[(b + 1) * L - 1:(b + 1) * L, :] for b in range(nseq)]
    cl_end = cl_last[nseq - 1]
    for b in range(nseq - 2, -1, -1):
        cl_end = jnp.where(row < (b + 1) * L, cl_last[b], cl_end)
    rt = r * jnp.exp(cl)
    at = avec * jnp.exp(cl - logw)
    e_neg = jnp.exp(-cl)
    kt = k2 * e_neg
    bt = bvec * e_neg
    e_rem = jnp.exp(cl_end - cl)
    kh = k2 * e_rem
    bh = bvec * e_rem
    p_last = [jnp.exp(cl_last[b]) for b in range(nseq)]
    rkk = r * k2 * rk_ref[...]

    row64 = lax.broadcasted_iota(jnp.int32, (L, L), 0)
    col64 = lax.broadcasted_iota(jnp.int32, (L, L), 1)
    strict = row64 > col64
    incl = row64 >= col64

    items = [(b, h) for b in range(nseq) for h in range(RWKV_HEADS)]
    n = range(len(items))
    rs = [slice(b * L, (b + 1) * L) for b, _ in items]
    hs = [slice(h * HEAD_DIM, (h + 1) * HEAD_DIM) for _, h in items]
    each = lambda f: [f(i) for i in n]
    mm = _dot1

    s0 = each(lambda i: s_scr[i])
    vh = each(lambda i: v[rs[i], hs[i]])
    ar = each(lambda i: jnp.concatenate([at[rs[i], hs[i]], rt[rs[i], hs[i]]], axis=0))
    mk = each(lambda i: mm(ar[i], kt[rs[i], hs[i]], NT))
    mb = each(lambda i: mm(ar[i], bt[rs[i], hs[i]], NT))
    m_ab = each(lambda i: jnp.where(strict, mb[i][:L], 0.0))
    m_ak = each(lambda i: jnp.where(strict, mk[i][:L], 0.0))
    m_rk = each(lambda i: jnp.where(incl, mk[i][L:], 0.0))
    m_rb = each(lambda i: jnp.where(incl, mb[i][L:], 0.0))

    eye = (row64 == col64).astype(F32)
    blk16 = (row64 >> 4) == (col64 >> 4)
    blk32 = (row64 >> 5) == (col64 >> 5)
    d1 = each(lambda i: jnp.where(blk16, m_ab[i], 0.0))
    e1 = each(lambda i: jnp.where(blk32 & jnp.logical_not(blk16), m_ab[i], 0.0))
    e2 = each(lambda i: jnp.where(jnp.logical_not(blk32), m_ab[i], 0.0))
    x = each(lambda i: eye + d1[i])
    d2 = each(lambda i: mm(d1[i], d1[i]))
    ars = each(lambda i: mm(ar[i], s0[i], NT))
    akv = each(lambda i: mm(m_ak[i], vh[i]))
    x = each(lambda i: x[i] + mm(d2[i], x[i]))
    d4 = each(lambda i: mm(d2[i], d2[i]))
    rkv = each(lambda i: mm(m_rk[i], vh[i]))
    x = each(lambda i: x[i] + mm(d4[i], x[i]))
    d8 = each(lambda i: mm(d4[i], d4[i]))
    svk = each(lambda i: mm(vh[i], kh[rs[i], hs[i]], TN))
    x = each(lambda i: x[i] + mm(d8[i], x[i]))
    t = each(lambda i: mm(e1[i], x[i]))
    x = each(lambda i: x[i] + mm(x[i], t[i]))
    t = each(lambda i: mm(e2[i], x[i]))
    x = each(lambda i: x[i] + mm(x[i], t[i]))
    u = each(lambda i: mm(x[i], ars[i][:L] + akv[i]))
    o = each(lambda i: ars[i][L:] + rkv[i] + mm(m_rb[i], u[i]))
    sub = each(lambda i: mm(u[i], bh[rs[i], hs[i]], TN))
    for i in n:
        s_scr[i] = s0[i] * p_last[items[i][0]][:, hs[i]] + svk[i] + sub[i]

    for i in n:
        mean = jnp.mean(o[i], axis=-1, keepdims=True)
        var = jnp.mean(jnp.square(o[i] - mean), axis=-1, keepdims=True)
        on = (o[i] - mean) * lax.rsqrt(var + GN_EPS) * lng_ref[:, hs[i]] + lnb_ref[:, hs[i]]
        bonus = jnp.sum(rkk[rs[i], hs[i]], axis=-1, keepdims=True) * vh[i]
        y_ref[items[i][0], :, hs[i]] = (on + bonus) * g[rs[i], hs[i]]


def _rwkv(proj, B, T, mu, w0, wup_pad, a0, aup_pad, gup, k_k, k_a, r_k, lng, lnb, seg):
    L = CHUNK
    nc = T // L
    nseq = RWKV_SEQS if B % RWKV_SEQS == 0 else 1
    proj3 = proj.reshape(B, T, N_MIX)
    vec = lambda n: pl.BlockSpec((1, n), lambda b, c: (0, 0))
    mat = lambda m, n: pl.BlockSpec((m, n), lambda b, c: (0, 0))
    y = pl.pallas_call(
        _rwkv_kernel,
        grid=(B // nseq, nc),
        in_specs=[
            pl.BlockSpec((nseq, L, RWKV_IN), lambda b, c: (b, c, 0)),
            pl.BlockSpec((nseq, 8, RWKV_IN), lambda b, c: (b, jnp.maximum(c * (L // 8) - 1, 0), 0)),
            vec(RWKV_IN), vec(WIDTH), mat(128, WIDTH), vec(WIDTH), mat(128, WIDTH), mat(GATE_LORA, WIDTH),
            vec(WIDTH), vec(WIDTH), vec(WIDTH), vec(WIDTH), vec(WIDTH), mat(WIDTH, WIDTH),
        ],
        out_specs=pl.BlockSpec((nseq, L, WIDTH), lambda b, c: (b, c, 0)),
        out_shape=jax.ShapeDtypeStruct((B, T, WIDTH), F32),
        scratch_shapes=[pltpu.VMEM((nseq * RWKV_HEADS, HEAD_DIM, HEAD_DIM), F32)],
        compiler_params=_params(("parallel", "arbitrary")),
        name="rwkv",
    )(proj3, proj3, mu, w0, wup_pad, a0, aup_pad, gup, k_k, k_a, r_k, lng, lnb, seg)
    return y.reshape(B * T, WIDTH)


def _bf16_split3(c):
    out, r = [], float(c)
    for _ in range(3):
        h = float(np.asarray(r, np.float32).astype(BF16).astype(np.float32))
        out.append(h)
        r -= h
    return out


LOG2E = float(np.log2(np.e))
Q_ROWS = 128
BIAS_ROW = HEAD_DIM
PEN_ROW = HEAD_DIM + 8
V_ROWS = HEAD_DIM + 16


def _attn_prep_kernel(q_ref, k_ref, v_ref, qg_ref, kg_ref, seg_ref, place_ref,
                      qt_out, k_out, vt_out, kmean_scr):
    i = pl.program_id(1)
    nbp = kmean_scr.shape[0]
    BLK = MOBA_BLOCK

    @pl.when(i == 0)
    def _():
        kmean_scr[...] = jnp.zeros_like(kmean_scr)

    inv_n = 1.0 / HEAD_DIM
    q = q_ref[...]
    k = k_ref[...]
    qn = q * lax.rsqrt(_dot2(q * q, seg_ref[...]) * inv_n + RMS_EPS) * qg_ref[...]
    kn = k * lax.rsqrt(_dot2(k * k, seg_ref[...]) * inv_n + RMS_EPS) * kg_ref[...]
    qt = (qn * (LOG2E * HEAD_DIM ** -0.5)).T

    kmean = jnp.mean(kn, axis=0, keepdims=True)
    rown = lax.broadcasted_iota(jnp.int32, kmean_scr.shape, 0)
    kmeans = jnp.where(rown == i, kmean, kmean_scr[...])
    kmean_scr[...] = kmeans

    kaug = _dot(kn.astype(BF16), place_ref[...])
    lane = lax.broadcasted_iota(jnp.int32, kaug.shape, 1) & 127
    pos = lax.broadcasted_iota(jnp.int32, kaug.shape, 0).astype(F32)
    kaug = jnp.where((lane >= BIAS_ROW) & (lane < BIAS_ROW + 3), pos, kaug)
    kaug = jnp.where((lane >= BIAS_ROW + 3) & (lane < BIAS_ROW + 6), i.astype(F32), kaug)
    kaug = jnp.where(lane == PEN_ROW + i, 1.0, kaug).astype(BF16)

    vt = v_ref[...].T
    sub16 = lax.broadcasted_iota(jnp.int32, (V_ROWS - HEAD_DIM, BLK), 0)
    ones_rows = jnp.where(sub16 == 0, 1.0, 0.0)

    blk_id = lax.broadcasted_iota(jnp.int32, (nbp, BLK), 0)
    sub8 = lax.broadcasted_iota(jnp.int32, (8, BLK), 0)
    tail = jnp.zeros((Q_ROWS - PEN_ROW - nbp, BLK), F32)
    for h in range(ATTN_HEADS):
        sl = slice(h * HEAD_DIM, (h + 1) * HEAD_DIM)
        qth = qt[sl, :]

        gate = _dot3(kmeans[:, sl], qth)
        gate = jnp.where(blk_id < i, gate, NEG_INF)
        chosen = blk_id == i
        for _ in range(MOBA_TOP_K):
            mx = jnp.max(gate, axis=0, keepdims=True)
            idx = jnp.min(jnp.where(gate == mx, blk_id, nbp), axis=0, keepdims=True)
            hit = blk_id == idx
            chosen = chosen | (hit & (blk_id < i))
            gate = jnp.where(hit, -jnp.inf, gate)
        penalty = jnp.where(chosen, 0.0, NEG_INF)

        slope = 2.0 ** (-8.0 * (h + 1) / ATTN_HEADS)
        cvals = _bf16_split3(slope * LOG2E) + _bf16_split3(slope * LOG2E * MOBA_BLOCK)
        consts = jnp.zeros((8, BLK), F32)
        for r, cv in enumerate(cvals):
            consts = jnp.where(sub8 == r, cv, consts)
        qt_out[0, h, 0] = jnp.concatenate([qth, consts, penalty, tail], axis=0).astype(BF16)
        k_out[0, h, 0] = kaug[:, h * 128:(h + 1) * 128]
        vt_out[0, h, 0] = jnp.concatenate([vt[sl, :], ones_rows], axis=0).astype(BF16)


def _attn_prep(proj, B, T, qg, kg, seg, place):
    nb = T // MOBA_BLOCK
    nbp = -(-nb // 8) * 8
    assert PEN_ROW + nbp <= Q_ROWS
    H = ATTN_HEADS
    col = lambda c: pl.BlockSpec((MOBA_BLOCK, WIDTH), lambda b, i: (b * nb + i, c))
    const = lambda m, n: pl.BlockSpec((m, n), lambda b, i: (0, 0))
    blk = lambda r, c: pl.BlockSpec((1, H, 1, r, c), lambda b, i: (b, 0, i, 0, 0))
    return pl.pallas_call(
        _attn_prep_kernel,
        grid=(B, nb),
        in_specs=[col(ATTN_COL), col(ATTN_COL + 1), col(ATTN_COL + 2),
                  const(1, WIDTH), const(1, WIDTH), const(WIDTH, WIDTH), const(WIDTH, 2 * WIDTH)],
        out_specs=[blk(Q_ROWS, MOBA_BLOCK), blk(MOBA_BLOCK, 128), blk(V_ROWS, MOBA_BLOCK)],
        out_shape=[
            jax.ShapeDtypeStruct((B, H, nb, Q_ROWS, MOBA_BLOCK), BF16),
            jax.ShapeDtypeStruct((B, H, nb, MOBA_BLOCK, 128), BF16),
            jax.ShapeDtypeStruct((B, H, nb, V_ROWS, MOBA_BLOCK), BF16),
        ],
        scratch_shapes=[pltpu.VMEM((nbp, WIDTH), F32)],
        compiler_params=_params(("parallel", "arbitrary")),
        name="attn_prep",
    )(proj, proj, proj, qg, kg, seg, place)


def _attn_kernel(q_ref, k_ref, v_ref, o_ref, sa_scr, sb_scr, m_scr, acc_scr):
    i = pl.program_id(1)
    BLK = MOBA_BLOCK
    rowk = lax.broadcasted_iota(jnp.int32, (BLK, BLK), 0)
    colq = lax.broadcasted_iota(jnp.int32, (BLK, BLK), 1)
    causal = rowk <= colq

    heads = range(ATTN_HEADS)
    each = lambda f: [f(h) for h in heads]
    q = each(lambda h: q_ref[0, h, 0])

    def softmax_update(h, s_h, j):
        m_old = m_scr[h]
        m_new = jnp.maximum(m_old, jnp.max(s_h, axis=0, keepdims=True))
        alpha = jnp.exp2(m_old - m_new)
        p = jnp.exp2(s_h - m_new).astype(BF16)
        m_scr[h] = m_new
        acc_scr[h] = acc_scr[h] * alpha + _dot(v_ref[0, h, j], p)

    def fused_step(dst, j_next, src, j):
        for h in heads:
            dst[h] = _dot(k_ref[0, h, j_next], q[h])
            softmax_update(h, src[h], j)

    for h in heads:
        sa_scr[h] = _dot(k_ref[0, h, 0], q[h])
        s_h = jnp.where(causal, _dot(k_ref[0, h, i], q[h]), NEG_INF)
        m0 = jnp.max(s_h, axis=0, keepdims=True)
        m_scr[h] = m0
        acc_scr[h] = _dot(v_ref[0, h, i], jnp.exp2(s_h - m0).astype(BF16))

    def two_blocks(jj, carry):
        j = 2 * jj
        fused_step(sb_scr, j + 1, sa_scr, j)
        fused_step(sa_scr, jnp.minimum(j + 2, i - 1), sb_scr, j + 1)
        return carry

    lax.fori_loop(0, i // 2, two_blocks, 0)

    @pl.when(i % 2 == 1)
    def _():
        for h in heads:
            softmax_update(h, sa_scr[h], i - 1)

    for h in heads:
        acc = acc_scr[h]
        o_ref[0, h * HEAD_DIM:(h + 1) * HEAD_DIM, :] = acc[:HEAD_DIM] / acc[HEAD_DIM:HEAD_DIM + 1]


def _attention(qt, kaug, vt, B, T):
    nb = T // MOBA_BLOCK
    H = ATTN_HEADS
    whole = lambda r, c: pl.BlockSpec((1, H, nb, r, c), lambda b, i: (b, 0, 0, 0, 0),
                                      pipeline_mode=pl.Buffered(1))
    return pl.pallas_call(
        _attn_kernel,
        grid=(B, nb),
        in_specs=[
            pl.BlockSpec((1, H, 1, Q_ROWS, MOBA_BLOCK), lambda b, i: (b, 0, i, 0, 0)),
            whole(MOBA_BLOCK, 128),
            whole(V_ROWS, MOBA_BLOCK),
        ],
        out_specs=pl.BlockSpec((1, WIDTH, MOBA_BLOCK), lambda b, i: (b, 0, i)),
        out_shape=jax.ShapeDtypeStruct((B, WIDTH, T), F32),
        scratch_shapes=[pltpu.VMEM((H, MOBA_BLOCK, MOBA_BLOCK), F32), pltpu.VMEM((H, MOBA_BLOCK, MOBA_BLOCK), F32),
                        pltpu.VMEM((H, 1, MOBA_BLOCK), F32), pltpu.VMEM((H, V_ROWS, MOBA_BLOCK), F32)],
        compiler_params=_params(("parallel", "arbitrary")),
        name="moba_attention",
    )(qt, kaug, vt)


POOL_HALO = 16


def _merge_kernel(x_ref, yr_ref, yat_ref, pp_ref, pprev_ref, g0_ref, g1_ref, g2_ref,
                  pw_ref, pscale_ref, wb_ref, wo_ref, o_ref):
    t = pl.program_id(1)
    tl = x_ref.shape[0]

    u = pp_ref[...]
    halo = jnp.where(t == 0, 0.0, pprev_ref[...])
    ext = jnp.concatenate([halo, u], axis=0)
    pos = (t * tl + lax.broadcasted_iota(jnp.int32, (tl, POOL_GROUP_DIM), 0) + 1).astype(F32)
    y_pool = []
    for gi, w in enumerate(POOL_WINDOWS):
        sl = slice(gi * POOL_GROUP_DIM, (gi + 1) * POOL_GROUP_DIM)
        acc = ext[:, sl]
        s = 1
        while s < w:
            acc = acc + pltpu.roll(acc, s, axis=0)
            s *= 2
        d = acc[POOL_HALO:] / jnp.minimum(pos, float(w)) - u[:, sl]
        y_pool.append(_dot(d.astype(BF16), pw_ref[gi]))
    y_pool = jnp.concatenate(y_pool, axis=1) * pscale_ref[...]

    br_r = _dot(yr_ref[...].astype(BF16), wb_ref[0])
    br_a = _dot(yat_ref[0].astype(BF16), wb_ref[1], TN)
    br_p = _dot(y_pool.astype(BF16), wb_ref[2])
    gate = lambda ref: jax.nn.sigmoid(ref[...].astype(F32))
    merged = gate(g0_ref) * br_r + gate(g1_ref) * br_a + gate(g2_ref) * br_p
    o_ref[...] = x_ref[...] + _dot(merged.astype(BF16), wo_ref[...])


def _merge(x2, y_rwkv, y_attn_t, proj, gates, B, T, pool_w, pool_scale, w_branch, w_out, tl=512):
    tl = min(tl, T)
    nt = T // tl
    rows = lambda w, c: pl.BlockSpec((tl, w), lambda b, t: (b * nt + t, c))
    const2 = lambda m, n: pl.BlockSpec((m, n), lambda b, t: (0, 0))
    const3 = lambda a, m, n: pl.BlockSpec((a, m, n), lambda b, t: (0, 0, 0))
    return pl.pallas_call(
        _merge_kernel,
        grid=(B, nt),
        in_specs=[
            rows(D_MODEL, 0),
            rows(WIDTH, 0),
            pl.BlockSpec((1, WIDTH, tl), lambda b, t: (b, 0, t)),
            rows(WIDTH, POOL_COL),
            pl.BlockSpec((POOL_HALO, WIDTH),
                         lambda b, t: (jnp.maximum((b * nt + t) * (tl // POOL_HALO) - 1, 0), POOL_COL)),
            rows(D_MODEL, 0), rows(D_MODEL, 1), rows(D_MODEL, 2),
            const3(len(POOL_WINDOWS), POOL_GROUP_DIM, POOL_GROUP_DIM),
            const2(1, WIDTH),
            const3(N_BRANCH, WIDTH, D_MODEL),
            const2(D_MODEL, D_MODEL),
        ],
        out_specs=rows(D_MODEL, 0),
        out_shape=jax.ShapeDtypeStruct((B * T, D_MODEL), F32),
        compiler_params=_params(("parallel", "arbitrary")),
        name="merge",
    )(x2, y_rwkv, y_attn_t, proj, proj, gates, gates, gates, pool_w, pool_scale, w_branch, w_out)


def _ffn_kernel(x_ref, g_ref, w1_ref, w2_ref, o_ref, h_scr):
    f = pl.program_id(1)

    @pl.when(f == 0)
    def _():
        x = x_ref[...]
        ms = jnp.mean(x * x, axis=-1, keepdims=True)
        h_scr[...] = (x * lax.rsqrt(ms + RMS_EPS) * g_ref[...]).astype(BF16)
        o_ref[...] = x

    a = jnp.maximum(_dot(h_scr[...], w1_ref[...]), 0.0)
    o_ref[...] += _dot((a * a).astype(BF16), w2_ref[...])


def _ffn(x2, g, w1, w2, tm=1024, tf=1024):
    m = x2.shape[0]
    tm = min(tm, m)
    return pl.pallas_call(
        _ffn_kernel,
        grid=(m // tm, D_FF // tf),
        in_specs=[
            pl.BlockSpec((tm, D_MODEL), lambda i, f: (i, 0)),
            pl.BlockSpec((1, D_MODEL), lambda i, f: (0, 0)),
            pl.BlockSpec((D_MODEL, tf), lambda i, f: (0, f)),
            pl.BlockSpec((tf, D_MODEL), lambda i, f: (f, 0)),
        ],
        out_specs=pl.BlockSpec((tm, D_MODEL), lambda i, f: (i, 0)),
        out_shape=jax.ShapeDtypeStruct((m, D_MODEL), F32),
        scratch_shapes=[pltpu.VMEM((tm, D_MODEL), BF16)],
        compiler_params=_params(("parallel", "arbitrary")),
        name="ffn",
    )(x2, g, w1, w2)


def _segment_ones():
    idx = np.arange(WIDTH) // HEAD_DIM
    return jnp.asarray(idx[:, None] == idx[None, :], BF16)


def _head_placement():
    src = np.arange(WIDTH)
    dst = (src // HEAD_DIM) * 128 + src % HEAD_DIM
    m = np.zeros((WIDTH, 2 * WIDTH), np.float32)
    m[src, dst] = 1.0
    return jnp.asarray(m, BF16)


def kernel(x, norm1_g, w_in, shift_mu, w0, w_up, a0, a_up, g_up, k_k, k_a, r_k, lnx_g, lnx_b,
           qn_g, kn_g, pool_w, pool_scale, w_branch, w_out, norm2_g, w_ff1, w_ff2):
    B, T, D = x.shape
    depth = w_in.shape[0]
    assert D == D_MODEL and T % MOBA_BLOCK == 0
    out_dtype = x.dtype
    x2 = x.astype(F32).reshape(B * T, D)
    seg = _segment_ones()
    place = _head_placement()
    row = lambda a: a.reshape(1, -1).astype(F32)
    zeros_lora = jnp.zeros((64, WIDTH), F32)

    for l in range(depth):
        w = w_in[l]
        w_mix = jnp.concatenate([w[:, :RWKV_IN], jnp.zeros((D, RWKV_PAD - RWKV_IN), w.dtype),
                                 w[:, RWKV_IN:RWKV_IN + 4 * WIDTH]], axis=1).astype(BF16)
        proj, gates = _inproj(x2, row(norm1_g[l]), w_mix, w[:, RWKV_IN + 4 * WIDTH:].astype(BF16))

        wup_pad = jnp.concatenate([w_up[l], zeros_lora], axis=0)
        aup_pad = jnp.concatenate([zeros_lora, a_up[l]], axis=0)
        y_rwkv = _rwkv(proj, B, T, row(shift_mu[l]), row(w0[l]), wup_pad, row(a0[l]), aup_pad, g_up[l],
                       row(k_k[l]), row(k_a[l]), row(r_k[l]), row(lnx_g[l]), row(lnx_b[l]), seg)

        qt, kaug, vt = _attn_prep(proj, B, T, row(jnp.tile(qn_g[l], ATTN_HEADS)),
                                  row(jnp.tile(kn_g[l], ATTN_HEADS)), seg, place)
        y_attn_t = _attention(qt, kaug, vt, B, T)

        x2 = _merge(x2, y_rwkv, y_attn_t, proj, gates, B, T, pool_w[l].astype(BF16), row(pool_scale[l]),
                    w_branch[l].astype(BF16), w_out[l].astype(BF16))
        x2 = _ffn(x2, row(norm2_g[l]), w_ff1[l].astype(BF16), w_ff2[l].astype(BF16))

    return x2.reshape(B, T, D).astype(out_dtype)
```

```python
import numpy as np
import jax
import jax.numpy as jnp
from jax import lax
from jax.experimental import pallas as pl
from jax.experimental.pallas import tpu as pltpu

F32 = jnp.float32
BF16 = jnp.bfloat16

D_MODEL = 1024
RWKV_HEADS = 8
HEAD_DIM = 64
WIDTH = RWKV_HEADS * HEAD_DIM
DECAY_LORA = 64
ICLR_LORA = 64
GATE_LORA = 128
ATTN_HEADS = 8
MOBA_BLOCK = 256
MOBA_TOP_K = 3
POOL_WINDOWS = (2, 4, 8, 16)
POOL_GROUP_DIM = 128
N_BRANCH = 3
D_FF = 4 * D_MODEL
RMS_EPS = 1e-6
GN_EPS = 64e-5
NEG_INF = -1e30

RWKV_IN = 3 * WIDTH + DECAY_LORA + ICLR_LORA + GATE_LORA
RWKV_PAD = 2048
ATTN_COL = RWKV_PAD // WIDTH
POOL_COL = ATTN_COL + 3
N_MIX = RWKV_PAD + 4 * WIDTH
N_GATE = N_BRANCH * D_MODEL

CHUNK = 64
RWKV_SEQS = 4
PACK = 4
VMEM_LIMIT = 56 * 1024 * 1024

NN = (((1,), (0,)), ((), ()))
NT = (((1,), (1,)), ((), ()))
TN = (((0,), (0,)), ((), ()))


def _dot(a, b, dims=NN):
    return lax.dot_general(a, b, dims, preferred_element_type=F32)


def _dot1(a, b, dims=NN):
    return _dot(a.astype(BF16), b.astype(BF16), dims)


def _split(a):
    hi = a.astype(BF16)
    lo = (a - hi.astype(F32)).astype(BF16)
    return hi, lo


def _dot3(a, b, dims=NN):
    ah, al = _split(a)
    bh, bl = _split(b)
    return _dot(ah, bh, dims) + (_dot(ah, bl, dims) + _dot(al, bh, dims))


def _dot2(a, b_bf16, dims=NN):
    ah, al = _split(a)
    return _dot(ah, b_bf16, dims) + _dot(al, b_bf16, dims)


def _params(sem):
    return pltpu.CompilerParams(dimension_semantics=sem, vmem_limit_bytes=VMEM_LIMIT)


def _inproj_kernel(x_ref, g_ref, wm_ref, wg_ref, om_ref, og_ref):
    x = x_ref[...]
    ms = jnp.mean(x * x, axis=-1, keepdims=True)
    h = (x * lax.rsqrt(ms + RMS_EPS) * g_ref[...]).astype(BF16)
    tn = D_MODEL
    for n in range(N_MIX // tn):
        om_ref[:, n * tn:(n + 1) * tn] = _dot(h, wm_ref[:, n * tn:(n + 1) * tn])
    for n in range(N_GATE // tn):
        og_ref[:, n * tn:(n + 1) * tn] = _dot(h, wg_ref[:, n * tn:(n + 1) * tn]).astype(BF16)


def _inproj(x2, g, w_mix, w_gate, tm=512):
    m = x2.shape[0]
    tm = min(tm, m)
    resident = lambda n: pl.BlockSpec((D_MODEL, n), lambda i: (0, 0), pipeline_mode=pl.Buffered(1))
    return pl.pallas_call(
        _inproj_kernel,
        grid=(m // tm,),
        in_specs=[
            pl.BlockSpec((tm, D_MODEL), lambda i: (i, 0)),
            pl.BlockSpec((1, D_MODEL), lambda i: (0, 0)),
            resident(N_MIX),
            resident(N_GATE),
        ],
        out_specs=[pl.BlockSpec((tm, N_MIX), lambda i: (i, 0)), pl.BlockSpec((tm, N_GATE), lambda i: (i, 0))],
        out_shape=[jax.ShapeDtypeStruct((m, N_MIX), F32), jax.ShapeDtypeStruct((m, N_GATE), BF16)],
        compiler_params=_params(("parallel",)),
        name="inproj",
    )(x2, g, w_mix, w_gate)


RWKV_STAGED = ("at", "rt", "kt", "bt", "kh", "bh", "v")


def _rwkv_kernel(p_ref, prev_ref, mu_ref, w0_ref, wup_ref, a0_ref, aup_ref, gup_ref,
                 kk_ref, ka_ref, rk_ref, lng_ref, lnb_ref, seg_ref, y_ref,
                 s_scr, at_scr, rt_scr, kt_scr, bt_scr, kh_scr, bh_scr, v_scr, g_scr, bv_scr, pl_scr):
    step = pl.program_id(1)
    nc = pl.num_programs(1) - 1
    L = CHUNK
    W = WIDTH
    nseq = p_ref.shape[0]
    R = nseq * L
    staged = dict(zip(RWKV_STAGED, (at_scr, rt_scr, kt_scr, bt_scr, kh_scr, bh_scr, v_scr)))

    @pl.when(step == 0)
    def _():
        for ref in (s_scr, g_scr, bv_scr, pl_scr) + tuple(staged.values()):
            ref[...] = jnp.zeros_like(ref)

    PW = PACK * HEAD_DIM
    rowp = lax.broadcasted_iota(jnp.int32, (L, PW), 0)
    lanep = lax.broadcasted_iota(jnp.int32, (L, PW), 1)
    colp = lanep & (HEAD_DIM - 1)
    head_of_lane = lanep >> 6
    strict = rowp > colp
    incl = rowp >= colp
    eye = (rowp == colp).astype(F32)
    blk16 = (rowp >> 4) == (colp >> 4)
    blk32 = (rowp >> 5) == (colp >> 5)

    def expand(x):
        xb = x.astype(BF16)
        zero = jnp.zeros_like(xb)
        return jnp.concatenate([jnp.where(head_of_lane == a, xb, zero) for a in range(PACK)], axis=0)

    def mm(a, x_exp):
        return _dot(a.astype(BF16), x_exp)

    def mm_nt(a, b_exp):
        return _dot(a.astype(BF16), b_exp, NT)

    def mm_tn(a, b):
        full = _dot(a.astype(BF16), b.astype(BF16), TN)
        out = jnp.where(head_of_lane == 0, full[0:HEAD_DIM], 0.0)
        for hh in range(1, PACK):
            out = out + jnp.where(head_of_lane == hh, full[hh * HEAD_DIM:(hh + 1) * HEAD_DIM], 0.0)
        return out

    first_chunk = jnp.minimum(step, nc - 1) == 0

    def mixed(c0, c1):
        pc = p_ref[:, :, c0:c1].reshape(R, c1 - c0)
        rows = lax.broadcasted_iota(jnp.int32, pc.shape, 0)
        shifted = pltpu.roll(pc, 1, axis=0)
        for b in range(nseq):
            first = jnp.where(first_chunk, 0.0, prev_ref[b, 7:8, c0:c1])
            shifted = jnp.where(rows == b * L, first, shifted)
        return pc + (shifted - pc) * mu_ref[:, c0:c1]

    def rows_like(x):
        return lax.broadcasted_iota(jnp.int32, x.shape, 0)

    pre = {}

    def prep_lora():
        lo = mixed(3 * W, 3 * W + 128)
        xg = mixed(3 * W + 128, 3 * W + 256)
        wpre = w0_ref[...] + _dot3(jnp.tanh(lo), wup_ref[...])
        pre["logw"] = -float(np.exp(-0.5)) * jax.nn.sigmoid(wpre)
        pre["a"] = jax.nn.sigmoid(a0_ref[...] + _dot1(lo, aup_ref[...]))
        g_scr[...] = _dot1(jax.nn.sigmoid(xg), gup_ref[...])

    def prep_cumsum():
        logw = pre["logw"]
        tl = rows_like(logw) & (L - 1)
        cl = logw
        sh = 1
        while sh < L:
            cl = cl + jnp.where(tl >= sh, pltpu.roll(cl, sh, axis=0), 0.0)
            sh *= 2
        cl_last = [cl[(b + 1) * L - 1:(b + 1) * L, :] for b in range(nseq)]
        cl_end = cl_last[nseq - 1]
        rows = rows_like(cl)
        for b in range(nseq - 2, -1, -1):
            cl_end = jnp.where(rows < (b + 1) * L, cl_last[b], cl_end)
        pre["cl"] = cl
        pre["e_neg"] = jnp.exp(-cl)
        pre["e_rem"] = jnp.exp(cl_end - cl)
        for b in range(nseq):
            pl_scr[b] = jnp.exp(cl_last[b])

    def prep_k():
        k = mixed(W, 2 * W)
        kk = k * kk_ref[...]
        ss = _dot2(kk * kk, seg_ref[...])
        pre["kk"] = kk * lax.rsqrt(jnp.maximum(ss, 1e-24))
        pre["k2"] = k * (1.0 + (pre["a"] - 1.0) * ka_ref[...])

    def prep_ab():
        bvec = pre["kk"] * pre["a"]
        at_scr[...] = (-pre["kk"] * jnp.exp(pre["cl"] - pre["logw"])).astype(BF16)
        bt_scr[...] = (bvec * pre["e_neg"]).astype(BF16)
        bh_scr[...] = (bvec * pre["e_rem"]).astype(BF16)

    def prep_kr():
        kt_scr[...] = (pre["k2"] * pre["e_neg"]).astype(BF16)
        kh_scr[...] = (pre["k2"] * pre["e_rem"]).astype(BF16)
        r = mixed(0, W)
        rt_scr[...] = (r * jnp.exp(pre["cl"])).astype(BF16)
        pre["rk"] = r * pre["k2"] * rk_ref[...]

    def prep_v():
        v = mixed(2 * W, 3 * W)
        v_scr[...] = v.astype(BF16)
        bv_scr[...] = _dot2(pre["rk"], seg_ref[...]) * v

    pieces = [prep_lora, prep_cumsum, prep_k, prep_ab, prep_kr, prep_v]

    def interleave():
        if pieces:
            pieces.pop(0)()

    items = [(b, gq) for b in range(nseq) for gq in range(RWKV_HEADS // PACK)]
    n = range(len(items))
    rs = [slice(b * L, (b + 1) * L) for b, _ in items]
    cs = [slice(gq * PW, (gq + 1) * PW) for _, gq in items]
    each = lambda f: [f(i) for i in n]
    ld = lambda name: each(lambda i: staged[name][rs[i], cs[i]])

    s0 = each(lambda i: jnp.where(step <= 1, 0.0, s_scr[i]))
    vh = ld("v")
    at, rt, kt, bt, kh, bh = ld("at"), ld("rt"), ld("kt"), ld("bt"), ld("kh"), ld("bh")
    gate = each(lambda i: g_scr[rs[i], cs[i]])
    bonus_v = each(lambda i: bv_scr[rs[i], cs[i]])
    p_last = each(lambda i: pl_scr[items[i][0], :, cs[i]])

    v_exp = each(lambda i: expand(vh[i]))
    ar = each(lambda i: jnp.concatenate([at[i], rt[i]], axis=0))
    mk = each(lambda i: mm_nt(ar[i], expand(kt[i])))
    mb = each(lambda i: mm_nt(ar[i], expand(bt[i])))
    m_ab = each(lambda i: jnp.where(strict, mb[i][:L], 0.0))
    m_kk = each(lambda i: jnp.concatenate([jnp.where(strict, mk[i][:L], 0.0),
                                           jnp.where(incl, mk[i][L:], 0.0)], axis=0))
    m_rb = each(lambda i: jnp.where(incl, mb[i][L:], 0.0))
    interleave()

    d1 = each(lambda i: jnp.where(blk16, m_ab[i], 0.0))
    e1 = each(lambda i: jnp.where(blk32 & jnp.logical_not(blk16), m_ab[i], 0.0))
    e2 = each(lambda i: jnp.where(jnp.logical_not(blk32), m_ab[i], 0.0))
    x = each(lambda i: eye + d1[i])
    d2 = each(lambda i: mm(d1[i], expand(d1[i])))
    ars = each(lambda i: mm_nt(ar[i], expand(s0[i])))
    kv = each(lambda i: mm(m_kk[i], v_exp[i]))
    interleave()
    x = each(lambda i: x[i] + mm(d2[i], expand(x[i])))
    d4 = each(lambda i: mm(d2[i], expand(d2[i])))
    interleave()
    x = each(lambda i: x[i] + mm(d4[i], expand(x[i])))
    d8 = each(lambda i: mm(d4[i], expand(d4[i])))
    interleave()
    x = each(lambda i: x[i] + mm(d8[i], expand(x[i])))
    t = each(lambda i: mm(e1[i], expand(x[i])))
    interleave()
    x = each(lambda i: x[i] + mm(x[i], expand(t[i])))
    t = each(lambda i: mm(e2[i], expand(x[i])))
    interleave()
    x = each(lambda i: x[i] + mm(x[i], expand(t[i])))
    u = each(lambda i: mm(x[i], expand(ars[i][:L] + kv[i][:L])))
    o = each(lambda i: ars[i][L:] + kv[i][L:] + mm(m_rb[i], expand(u[i])))
    s_add = each(lambda i: mm_tn(jnp.concatenate([vh[i], u[i].astype(BF16)], axis=0),
                                 jnp.concatenate([kh[i], bh[i]], axis=0)))
    for i in n:
        s_scr[i] = s0[i] * p_last[i] + s_add[i]

    seg = seg_ref[0:PW, 0:PW]
    inv_n = 1.0 / HEAD_DIM
    for i in n:
        mean = _dot2(o[i], seg) * inv_n
        cen = o[i] - mean
        var = _dot2(cen * cen, seg) * inv_n
        on = cen * lax.rsqrt(var + GN_EPS) * lng_ref[:, cs[i]] + lnb_ref[:, cs[i]]
        y_ref[items[i][0], :, cs[i]] = (on + bonus_v[i]) * gate[i]
    while pieces:
        interleave()


def _rwkv(proj, B, T, mu, w0, wup_pad, a0, aup_pad, gup, k_k, k_a, r_k, lng, lnb, seg):
    L = CHUNK
    nc = T // L
    nseq = RWKV_SEQS if B % RWKV_SEQS == 0 else 1
    R = nseq * L
    proj3 = proj.reshape(B, T, N_MIX)
    vec = lambda n: pl.BlockSpec((1, n), lambda b, s: (0, 0))
    mat = lambda m, n: pl.BlockSpec((m, n), lambda b, s: (0, 0))
    cur = lambda s: jnp.minimum(s, nc - 1)
    y = pl.pallas_call(
        _rwkv_kernel,
        grid=(B // nseq, nc + 1),
        in_specs=[
            pl.BlockSpec((nseq, L, RWKV_IN), lambda b, s: (b, cur(s), 0)),
            pl.BlockSpec((nseq, 8, RWKV_IN), lambda b, s: (b, jnp.maximum(cur(s) * (L // 8) - 1, 0), 0)),
            vec(RWKV_IN), vec(WIDTH), mat(128, WIDTH), vec(WIDTH), mat(128, WIDTH), mat(GATE_LORA, WIDTH),
            vec(WIDTH), vec(WIDTH), vec(WIDTH), vec(WIDTH), vec(WIDTH), mat(WIDTH, WIDTH),
        ],
        out_specs=pl.BlockSpec((nseq, L, WIDTH), lambda b, s: (b, jnp.maximum(s - 1, 0), 0)),
        out_shape=jax.ShapeDtypeStruct((B, T, WIDTH), F32),
        scratch_shapes=[pltpu.VMEM((nseq * RWKV_HEADS // PACK, HEAD_DIM, PACK * HEAD_DIM), F32)]
        + [pltpu.VMEM((R, WIDTH), BF16)] * len(RWKV_STAGED)
        + [pltpu.VMEM((R, WIDTH), F32), pltpu.VMEM((R, WIDTH), F32), pltpu.VMEM((nseq, 1, WIDTH), F32)],
        compiler_params=_params(("parallel", "arbitrary")),
        name="rwkv",
    )(proj3, proj3, mu, w0, wup_pad, a0, aup_pad, gup, k_k, k_a, r_k, lng, lnb, seg)
    return y.reshape(B * T, WIDTH)


def _bf16_split3(c):
    out, r = [], float(c)
    for _ in range(3):
        h = float(np.asarray(r, np.float32).astype(BF16).astype(np.float32))
        out.append(h)
        r -= h
    return out


LOG2E = float(np.log2(np.e))
Q_ROWS = 128
BIAS_ROW = HEAD_DIM
PEN_ROW = HEAD_DIM + 8
V_ROWS = HEAD_DIM + 16


def _attn_prep_kernel(q_ref, k_ref, v_ref, qg_ref, kg_ref, seg_ref, place_ref,
                      qt_out, k_out, vt_out, kmean_scr):
    i = pl.program_id(1)
    nbp = kmean_scr.shape[0]
    BLK = MOBA_BLOCK

    @pl.when(i == 0)
    def _():
        kmean_scr[...] = jnp.zeros_like(kmean_scr)

    inv_n = 1.0 / HEAD_DIM
    q = q_ref[...]
    k = k_ref[...]
    qn = q * lax.rsqrt(_dot2(q * q, seg_ref[...]) * inv_n + RMS_EPS) * qg_ref[...]
    kn = k * lax.rsqrt(_dot2(k * k, seg_ref[...]) * inv_n + RMS_EPS) * kg_ref[...]
    qt = (qn * (LOG2E * HEAD_DIM ** -0.5)).T

    kmean = jnp.mean(kn, axis=0, keepdims=True)
    rown = lax.broadcasted_iota(jnp.int32, kmean_scr.shape, 0)
    kmeans = jnp.where(rown == i, kmean, kmean_scr[...])
    kmean_scr[...] = kmeans

    kaug = _dot(kn.astype(BF16), place_ref[...])
    lane = lax.broadcasted_iota(jnp.int32, kaug.shape, 1) & 127
    pos = lax.broadcasted_iota(jnp.int32, kaug.shape, 0).astype(F32)
    kaug = jnp.where((lane >= BIAS_ROW) & (lane < BIAS_ROW + 3), pos, kaug)
    kaug = jnp.where((lane >= BIAS_ROW + 3) & (lane < BIAS_ROW + 6), i.astype(F32), kaug)
    kaug = jnp.where(lane == PEN_ROW + i, 1.0, kaug).astype(BF16)

    vt = v_ref[...].T
    sub16 = lax.broadcasted_iota(jnp.int32, (V_ROWS - HEAD_DIM, BLK), 0)
    ones_rows = jnp.where(sub16 == 0, 1.0, 0.0)

    blk_id = lax.broadcasted_iota(jnp.int32, (nbp, BLK), 0)
    sub8 = lax.broadcasted_iota(jnp.int32, (8, BLK), 0)
    tail = jnp.zeros((Q_ROWS - PEN_ROW - nbp, BLK), F32)
    for h in range(ATTN_HEADS):
        sl = slice(h * HEAD_DIM, (h + 1) * HEAD_DIM)
        qth = qt[sl, :]

        gate = _dot3(kmeans[:, sl], qth)
        gate = jnp.where(blk_id < i, gate, NEG_INF)
        chosen = blk_id == i
        for _ in range(MOBA_TOP_K):
            mx = jnp.max(gate, axis=0, keepdims=True)
            idx = jnp.min(jnp.where(gate == mx, blk_id, nbp), axis=0, keepdims=True)
            hit = blk_id == idx
            chosen = chosen | (hit & (blk_id < i))
            gate = jnp.where(hit, -jnp.inf, gate)
        penalty = jnp.where(chosen, 0.0, NEG_INF)

        slope = 2.0 ** (-8.0 * (h + 1) / ATTN_HEADS)
        cvals = _bf16_split3(slope * LOG2E) + _bf16_split3(slope * LOG2E * MOBA_BLOCK)
        consts = jnp.zeros((8, BLK), F32)
        for r, cv in enumerate(cvals):
            consts = jnp.where(sub8 == r, cv, consts)
        qt_out[0, h, 0] = jnp.concatenate([qth, consts, penalty, tail], axis=0).astype(BF16)
        k_out[0, h, 0] = kaug[:, h * 128:(h + 1) * 128]
        vt_out[0, h, 0] = jnp.concatenate([vt[sl, :], ones_rows], axis=0).astype(BF16)


def _attn_prep(proj, B, T, qg, kg, seg, place):
    nb = T // MOBA_BLOCK
    nbp = -(-nb // 8) * 8
    assert PEN_ROW + nbp <= Q_ROWS
    H = ATTN_HEADS
    col = lambda c: pl.BlockSpec((MOBA_BLOCK, WIDTH), lambda b, i: (b * nb + i, c))
    const = lambda m, n: pl.BlockSpec((m, n), lambda b, i: (0, 0))
    blk = lambda r, c: pl.BlockSpec((1, H, 1, r, c), lambda b, i: (b, 0, i, 0, 0))
    return pl.pallas_call(
        _attn_prep_kernel,
        grid=(B, nb),
        in_specs=[col(ATTN_COL), col(ATTN_COL + 1), col(ATTN_COL + 2),
                  const(1, WIDTH), const(1, WIDTH), const(WIDTH, WIDTH), const(WIDTH, 2 * WIDTH)],
        out_specs=[blk(Q_ROWS, MOBA_BLOCK), blk(MOBA_BLOCK, 128), blk(V_ROWS, MOBA_BLOCK)],
        out_shape=[
            jax.ShapeDtypeStruct((B, H, nb, Q_ROWS, MOBA_BLOCK), BF16),
            jax.ShapeDtypeStruct((B, H, nb, MOBA_BLOCK, 128), BF16),
            jax.ShapeDtypeStruct((B, H, nb, V_ROWS, MOBA_BLOCK), BF16),
        ],
        scratch_shapes=[pltpu.VMEM((nbp, WIDTH), F32)],
        compiler_params=_params(("parallel", "arbitrary")),
        name="attn_prep",
    )(proj, proj, proj, qg, kg, seg, place)


def _attn_kernel(q_ref, k_ref, v_ref, o_ref, sa_scr, sb_scr, ma_scr, mb_scr, m_scr, acc_scr):
    i = pl.program_id(1)
    BLK = MOBA_BLOCK
    rowk = lax.broadcasted_iota(jnp.int32, (BLK, BLK), 0)
    colq = lax.broadcasted_iota(jnp.int32, (BLK, BLK), 1)
    causal = rowk <= colq

    heads = range(ATTN_HEADS)
    q = [q_ref[0, h, 0] for h in heads]

    def scores_into(dst, h, j):
        s_h = _dot(k_ref[0, h, j], q[h])
        dst[0][h] = s_h
        dst[1][h] = jnp.max(s_h, axis=0, keepdims=True)

    def softmax_update(h, src, j):
        m_old = m_scr[h]
        m_new = jnp.maximum(m_old, src[1][h])
        alpha = jnp.exp2(m_old - m_new)
        p = jnp.exp2(src[0][h] - m_new).astype(BF16)
        m_scr[h] = m_new
        acc_scr[h] = acc_scr[h] * alpha + _dot(v_ref[0, h, j], p)

    def fused_step(dst, j_next, src, j):
        for h in heads:
            scores_into(dst, h, j_next)
            softmax_update(h, src, j)

    buf_a = (sa_scr, ma_scr)
    buf_b = (sb_scr, mb_scr)

    own = lambda h: _dot(k_ref[0, h, i], q[h])
    s_next = own(0)
    for h in heads:
        s_h = jnp.where(causal, s_next, NEG_INF)
        if h + 1 < ATTN_HEADS:
            s_next = own(h + 1)
        scores_into(buf_a, h, 0)
        m0 = jnp.max(s_h, axis=0, keepdims=True)
        m_scr[h] = m0
        acc_scr[h] = _dot(v_ref[0, h, i], jnp.exp2(s_h - m0).astype(BF16))

    def two_blocks(jj, carry):
        j = 2 * jj
        fused_step(buf_b, j + 1, buf_a, j)
        fused_step(buf_a, jnp.minimum(j + 2, i - 1), buf_b, j + 1)
        return carry

    lax.fori_loop(0, i // 2, two_blocks, 0)

    @pl.when(i % 2 == 1)
    def _():
        for h in heads:
            softmax_update(h, buf_a, i - 1)

    for h in heads:
        acc = acc_scr[h]
        o_ref[0, h * HEAD_DIM:(h + 1) * HEAD_DIM, :] = acc[:HEAD_DIM] / acc[HEAD_DIM:HEAD_DIM + 1]


def _attention(qt, kaug, vt, B, T):
    nb = T // MOBA_BLOCK
    H = ATTN_HEADS
    whole = lambda r, c: pl.BlockSpec((1, H, nb, r, c), lambda b, i: (b, 0, 0, 0, 0),
                                      pipeline_mode=pl.Buffered(1))
    tile = lambda r: pltpu.VMEM((H, r, MOBA_BLOCK), F32)
    return pl.pallas_call(
        _attn_kernel,
        grid=(B, nb),
        in_specs=[
            pl.BlockSpec((1, H, 1, Q_ROWS, MOBA_BLOCK), lambda b, i: (b, 0, i, 0, 0)),
            whole(MOBA_BLOCK, 128),
            whole(V_ROWS, MOBA_BLOCK),
        ],
        out_specs=pl.BlockSpec((1, WIDTH, MOBA_BLOCK), lambda b, i: (b, 0, i)),
        out_shape=jax.ShapeDtypeStruct((B, WIDTH, T), F32),
        scratch_shapes=[tile(MOBA_BLOCK), tile(MOBA_BLOCK), tile(1), tile(1), tile(1), tile(V_ROWS)],
        compiler_params=_params(("parallel", "arbitrary")),
        name="moba_attention",
    )(qt, kaug, vt)


POOL_HALO = 16


def _merge_kernel(x_ref, yr_ref, yat_ref, pp_ref, pprev_ref, g0_ref, g1_ref, g2_ref,
                  pw_ref, pscale_ref, wb_ref, wo_ref, o_ref):
    t = pl.program_id(1)
    tl = x_ref.shape[0]

    u = pp_ref[...]
    halo = jnp.where(t == 0, 0.0, pprev_ref[...])
    ext = jnp.concatenate([halo, u], axis=0)
    pos = (t * tl + lax.broadcasted_iota(jnp.int32, (tl, POOL_GROUP_DIM), 0) + 1).astype(F32)
    y_pool = []
    for gi, w in enumerate(POOL_WINDOWS):
        sl = slice(gi * POOL_GROUP_DIM, (gi + 1) * POOL_GROUP_DIM)
        acc = ext[:, sl]
        s = 1
        while s < w:
            acc = acc + pltpu.roll(acc, s, axis=0)
            s *= 2
        d = acc[POOL_HALO:] / jnp.minimum(pos, float(w)) - u[:, sl]
        y_pool.append(_dot(d.astype(BF16), pw_ref[gi]))
    y_pool = jnp.concatenate(y_pool, axis=1) * pscale_ref[...]

    br_r = _dot(yr_ref[...].astype(BF16), wb_ref[0])
    br_a = _dot(yat_ref[0].astype(BF16), wb_ref[1], TN)
    br_p = _dot(y_pool.astype(BF16), wb_ref[2])
    gate = lambda ref: jax.nn.sigmoid(ref[...].astype(F32))
    merged = gate(g0_ref) * br_r + gate(g1_ref) * br_a + gate(g2_ref) * br_p
    o_ref[...] = x_ref[...] + _dot(merged.astype(BF16), wo_ref[...])


def _merge(x2, y_rwkv, y_attn_t, proj, gates, B, T, pool_w, pool_scale, w_branch, w_out, tl=512):
    tl = min(tl, T)
    nt = T // tl
    rows = lambda w, c: pl.BlockSpec((tl, w), lambda b, t: (b * nt + t, c))
    const2 = lambda m, n: pl.BlockSpec((m, n), lambda b, t: (0, 0))
    const3 = lambda a, m, n: pl.BlockSpec((a, m, n), lambda b, t: (0, 0, 0))
    return pl.pallas_call(
        _merge_kernel,
        grid=(B, nt),
        in_specs=[
            rows(D_MODEL, 0),
            rows(WIDTH, 0),
            pl.BlockSpec((1, WIDTH, tl), lambda b, t: (b, 0, t)),
            rows(WIDTH, POOL_COL),
            pl.BlockSpec((POOL_HALO, WIDTH),
                         lambda b, t: (jnp.maximum((b * nt + t) * (tl // POOL_HALO) - 1, 0), POOL_COL)),
            rows(D_MODEL, 0), rows(D_MODEL, 1), rows(D_MODEL, 2),
            const3(len(POOL_WINDOWS), POOL_GROUP_DIM, POOL_GROUP_DIM),
            const2(1, WIDTH),
            const3(N_BRANCH, WIDTH, D_MODEL),
            const2(D_MODEL, D_MODEL),
        ],
        out_specs=rows(D_MODEL, 0),
        out_shape=jax.ShapeDtypeStruct((B * T, D_MODEL), F32),
        compiler_params=_params(("parallel", "arbitrary")),
        name="merge",
    )(x2, y_rwkv, y_attn_t, proj, proj, gates, gates, gates, pool_w, pool_scale, w_branch, w_out)


def _ffn_kernel(x_ref, g_ref, w1_ref, w2_ref, o_ref, h_scr):
    f = pl.program_id(1)

    @pl.when(f == 0)
    def _():
        x = x_ref[...]
        ms = jnp.mean(x * x, axis=-1, keepdims=True)
        h_scr[...] = (x * lax.rsqrt(ms + RMS_EPS) * g_ref[...]).astype(BF16)
        o_ref[...] = x

    a = jnp.maximum(_dot(h_scr[...], w1_ref[...]), 0.0)
    o_ref[...] += _dot((a * a).astype(BF16), w2_ref[...])


def _ffn(x2, g, w1, w2, tm=1024, tf=1024):
    m = x2.shape[0]
    tm = min(tm, m)
    return pl.pallas_call(
        _ffn_kernel,
        grid=(m // tm, D_FF // tf),
        in_specs=[
            pl.BlockSpec((tm, D_MODEL), lambda i, f: (i, 0)),
            pl.BlockSpec((1, D_MODEL), lambda i, f: (0, 0)),
            pl.BlockSpec((D_MODEL, tf), lambda i, f: (0, f)),
            pl.BlockSpec((tf, D_MODEL), lambda i, f: (f, 0)),
        ],
        out_specs=pl.BlockSpec((tm, D_MODEL), lambda i, f: (i, 0)),
        out_shape=jax.ShapeDtypeStruct((m, D_MODEL), F32),
        scratch_shapes=[pltpu.VMEM((tm, D_MODEL), BF16)],
        compiler_params=_params(("parallel", "arbitrary")),
        name="ffn",
    )(x2, g, w1, w2)


def _segment_ones():
    idx = np.arange(WIDTH) // HEAD_DIM
    return jnp.asarray(idx[:, None] == idx[None, :], BF16)


def _head_placement():
    src = np.arange(WIDTH)
    dst = (src // HEAD_DIM) * 128 + src % HEAD_DIM
    m = np.zeros((WIDTH, 2 * WIDTH), np.float32)
    m[src, dst] = 1.0
    return jnp.asarray(m, BF16)


def kernel(x, norm1_g, w_in, shift_mu, w0, w_up, a0, a_up, g_up, k_k, k_a, r_k, lnx_g, lnx_b,
           qn_g, kn_g, pool_w, pool_scale, w_branch, w_out, norm2_g, w_ff1, w_ff2):
    B, T, D = x.shape
    depth = w_in.shape[0]
    assert D == D_MODEL and T % MOBA_BLOCK == 0
    out_dtype = x.dtype
    x2 = x.astype(F32).reshape(B * T, D)
    seg = _segment_ones()
    place = _head_placement()
    row = lambda a: a.reshape(1, -1).astype(F32)
    zeros_lora = jnp.zeros((64, WIDTH), F32)

    for l in range(depth):
        w = w_in[l]
        w_mix = jnp.concatenate([w[:, :RWKV_IN], jnp.zeros((D, RWKV_PAD - RWKV_IN), w.dtype),
                                 w[:, RWKV_IN:RWKV_IN + 4 * WIDTH]], axis=1).astype(BF16)
        proj, gates = _inproj(x2, row(norm1_g[l]), w_mix, w[:, RWKV_IN + 4 * WIDTH:].astype(BF16))

        wup_pad = jnp.concatenate([w_up[l], zeros_lora], axis=0)
        aup_pad = jnp.concatenate([zeros_lora, a_up[l]], axis=0)
        y_rwkv = _rwkv(proj, B, T, row(shift_mu[l]), row(w0[l]), wup_pad, row(a0[l]), aup_pad, g_up[l],
                       row(k_k[l]), row(k_a[l]), row(r_k[l]), row(lnx_g[l]), row(lnx_b[l]), seg)

        qt, kaug, vt = _attn_prep(proj, B, T, row(jnp.tile(qn_g[l], ATTN_HEADS)),
                                  row(jnp.tile(kn_g[l], ATTN_HEADS)), seg, place)
        y_attn_t = _attention(qt, kaug, vt, B, T)

        x2 = _merge(x2, y_rwkv, y_attn_t, proj, gates, B, T, pool_w[l].astype(BF16), row(pool_scale[l]),
                    w_branch[l].astype(BF16), w_out[l].astype(BF16))
        x2 = _ffn(x2, row(norm2_g[l]), w_ff1[l].astype(BF16), w_ff2[l].astype(BF16))

    return x2.reshape(B, T, D).astype(out_dtype)
```

```python
import numpy as np
import jax
import jax.numpy as jnp
from jax import lax
from jax.experimental import pallas as pl
from jax.experimental.pallas import tpu as pltpu

F32 = jnp.float32
BF16 = jnp.bfloat16

D_MODEL = 1024
RWKV_HEADS = 8
HEAD_DIM = 64
WIDTH = RWKV_HEADS * HEAD_DIM
DECAY_LORA = 64
ICLR_LORA = 64
GATE_LORA = 128
ATTN_HEADS = 8
MOBA_BLOCK = 256
MOBA_TOP_K = 3
POOL_WINDOWS = (2, 4, 8, 16)
POOL_GROUP_DIM = 128
N_BRANCH = 3
D_FF = 4 * D_MODEL
RMS_EPS = 1e-6
GN_EPS = 64e-5
NEG_INF = -1e30

RWKV_IN = 3 * WIDTH + DECAY_LORA + ICLR_LORA + GATE_LORA
RWKV_PAD = 2048
ATTN_COL = RWKV_PAD // WIDTH
POOL_COL = ATTN_COL + 3
N_MIX = RWKV_PAD + 4 * WIDTH
N_GATE = N_BRANCH * D_MODEL

CHUNK = 64
RWKV_SEQS = 4
PACK = 4
VMEM_LIMIT = 56 * 1024 * 1024

NN = (((1,), (0,)), ((), ()))
NT = (((1,), (1,)), ((), ()))
TN = (((0,), (0,)), ((), ()))


def _dot(a, b, dims=NN):
    return lax.dot_general(a, b, dims, preferred_element_type=F32)


def _dot1(a, b, dims=NN):
    return _dot(a.astype(BF16), b.astype(BF16), dims)


def _split(a):
    hi = a.astype(BF16)
    lo = (a - hi.astype(F32)).astype(BF16)
    return hi, lo


def _dot3(a, b, dims=NN):
    ah, al = _split(a)
    bh, bl = _split(b)
    return _dot(ah, bh, dims) + (_dot(ah, bl, dims) + _dot(al, bh, dims))


def _params(sem):
    return pltpu.CompilerParams(dimension_semantics=sem, vmem_limit_bytes=VMEM_LIMIT)


def _inproj_kernel(x_ref, g_ref, wm_ref, wg_ref, om_ref, og_ref):
    x = x_ref[...]
    ms = jnp.mean(x * x, axis=-1, keepdims=True)
    h = (x * lax.rsqrt(ms + RMS_EPS) * g_ref[...]).astype(BF16)
    tn = D_MODEL
    for n in range(N_MIX // tn):
        om_ref[:, n * tn:(n + 1) * tn] = _dot(h, wm_ref[:, n * tn:(n + 1) * tn])
    for n in range(N_GATE // tn):
        og_ref[:, n * tn:(n + 1) * tn] = _dot(h, wg_ref[:, n * tn:(n + 1) * tn]).astype(BF16)


def _inproj(x2, g, w_mix, w_gate, tm=512):
    m = x2.shape[0]
    tm = min(tm, m)
    resident = lambda n: pl.BlockSpec((D_MODEL, n), lambda i: (0, 0), pipeline_mode=pl.Buffered(1))
    return pl.pallas_call(
        _inproj_kernel,
        grid=(m // tm,),
        in_specs=[
            pl.BlockSpec((tm, D_MODEL), lambda i: (i, 0)),
            pl.BlockSpec((1, D_MODEL), lambda i: (0, 0)),
            resident(N_MIX),
            resident(N_GATE),
        ],
        out_specs=[pl.BlockSpec((tm, N_MIX), lambda i: (i, 0)), pl.BlockSpec((tm, N_GATE), lambda i: (i, 0))],
        out_shape=[jax.ShapeDtypeStruct((m, N_MIX), F32), jax.ShapeDtypeStruct((m, N_GATE), BF16)],
        compiler_params=_params(("parallel",)),
        name="inproj",
    )(x2, g, w_mix, w_gate)


RWKV_STAGED = ("at", "rt", "kt", "bt", "kh", "bh", "v")


def _rwkv_kernel(p_ref, prev_ref, mu_ref, w0_ref, wup_ref, a0_ref, aup_ref, gup_ref,
                 kk_ref, ka_ref, rk_ref, lng_ref, lnb_ref, seg_ref, y_ref,
                 s_scr, at_scr, rt_scr, kt_scr, bt_scr, kh_scr, bh_scr, v_scr, g_scr, bv_scr, pl_scr):
    step = pl.program_id(1)
    nc = pl.num_programs(1) - 1
    L = CHUNK
    W = WIDTH
    nseq = p_ref.shape[0]
    R = nseq * L
    staged = dict(zip(RWKV_STAGED, (at_scr, rt_scr, kt_scr, bt_scr, kh_scr, bh_scr, v_scr)))

    @pl.when(step == 0)
    def _():
        for ref in (s_scr, g_scr, bv_scr, pl_scr) + tuple(staged.values()):
            ref[...] = jnp.zeros_like(ref)

    PW = PACK * HEAD_DIM
    rowp = lax.broadcasted_iota(jnp.int32, (L, PW), 0)
    lanep = lax.broadcasted_iota(jnp.int32, (L, PW), 1)
    colp = lanep & (HEAD_DIM - 1)
    head_of_lane = lanep >> 6
    strict = rowp > colp
    incl = rowp >= colp
    eye = (rowp == colp).astype(F32)
    blk16 = (rowp >> 4) == (colp >> 4)
    blk32 = (rowp >> 5) == (colp >> 5)

    def expand(x):
        xb = x.astype(BF16)
        zero = jnp.zeros_like(xb)
        return jnp.concatenate([jnp.where(head_of_lane == a, xb, zero) for a in range(PACK)], axis=0)

    def mm(a, x_exp):
        return _dot(a.astype(BF16), x_exp)

    def mm_nt(a, b_exp):
        return _dot(a.astype(BF16), b_exp, NT)

    def mm_tn(a, b):
        full = _dot(a.astype(BF16), b.astype(BF16), TN)
        out = jnp.where(head_of_lane == 0, full[0:HEAD_DIM], 0.0)
        for hh in range(1, PACK):
            out = out + jnp.where(head_of_lane == hh, full[hh * HEAD_DIM:(hh + 1) * HEAD_DIM], 0.0)
        return out

    first_chunk = jnp.minimum(step, nc - 1) == 0

    def mixed(c0, c1):
        pc = p_ref[:, :, c0:c1].reshape(R, c1 - c0)
        rows = lax.broadcasted_iota(jnp.int32, pc.shape, 0)
        shifted = pltpu.roll(pc, 1, axis=0)
        for b in range(nseq):
            first = jnp.where(first_chunk, 0.0, prev_ref[b, 7:8, c0:c1])
            shifted = jnp.where(rows == b * L, first, shifted)
        return pc + (shifted - pc) * mu_ref[:, c0:c1]

    def rows_like(x):
        return lax.broadcasted_iota(jnp.int32, x.shape, 0)

    pre = {}

    def prep_lora():
        lo = mixed(3 * W, 3 * W + 128)
        xg = mixed(3 * W + 128, 3 * W + 256)
        wpre = w0_ref[...] + _dot3(jnp.tanh(lo), wup_ref[...])
        pre["logw"] = -float(np.exp(-0.5)) * jax.nn.sigmoid(wpre)
        pre["a"] = jax.nn.sigmoid(a0_ref[...] + _dot1(lo, aup_ref[...]))
        g_scr[...] = _dot1(jax.nn.sigmoid(xg), gup_ref[...])

    def prep_cumsum():
        logw = pre["logw"]
        tl = rows_like(logw) & (L - 1)
        cl = logw
        sh = 1
        while sh < L:
            cl = cl + jnp.where(tl >= sh, pltpu.roll(cl, sh, axis=0), 0.0)
            sh *= 2
        cl_last = [cl[(b + 1) * L - 1:(b + 1) * L, :] for b in range(nseq)]
        cl_end = cl_last[nseq - 1]
        rows = rows_like(cl)
        for b in range(nseq - 2, -1, -1):
            cl_end = jnp.where(rows < (b + 1) * L, cl_last[b], cl_end)
        pre["cl"] = cl
        pre["e_neg"] = jnp.exp(-cl)
        pre["e_rem"] = jnp.exp(cl_end - cl)
        for b in range(nseq):
            pl_scr[b] = jnp.exp(cl_last[b])

    def prep_k():
        k = mixed(W, 2 * W)
        kk = k * kk_ref[...]
        ss = _dot1(kk * kk, seg_ref[...])
        pre["kk"] = kk * lax.rsqrt(jnp.maximum(ss, 1e-24))
        pre["k2"] = k * (1.0 + (pre["a"] - 1.0) * ka_ref[...])

    def prep_ab():
        bvec = pre["kk"] * pre["a"]
        at_scr[...] = (-pre["kk"] * jnp.exp(pre["cl"] - pre["logw"])).astype(BF16)
        bt_scr[...] = (bvec * pre["e_neg"]).astype(BF16)
        bh_scr[...] = (bvec * pre["e_rem"]).astype(BF16)

    def prep_kr():
        kt_scr[...] = (pre["k2"] * pre["e_neg"]).astype(BF16)
        kh_scr[...] = (pre["k2"] * pre["e_rem"]).astype(BF16)
        r = mixed(0, W)
        rt_scr[...] = (r * jnp.exp(pre["cl"])).astype(BF16)
        pre["rk"] = r * pre["k2"] * rk_ref[...]

    def prep_v():
        v = mixed(2 * W, 3 * W)
        v_scr[...] = v.astype(BF16)
        bv_scr[...] = _dot1(pre["rk"], seg_ref[...]) * v

    pieces = [prep_lora, prep_cumsum, prep_k, prep_ab, prep_kr, prep_v]

    def interleave():
        if pieces:
            pieces.pop(0)()

    items = [(b, gq) for b in range(nseq) for gq in range(RWKV_HEADS // PACK)]
    n = range(len(items))
    rs = [slice(b * L, (b + 1) * L) for b, _ in items]
    cs = [slice(gq * PW, (gq + 1) * PW) for _, gq in items]
    each = lambda f: [f(i) for i in n]
    ld = lambda name: each(lambda i: staged[name][rs[i], cs[i]])

    s0 = each(lambda i: jnp.where(step <= 1, 0.0, s_scr[i]))
    vh = ld("v")
    at, rt, kt, bt, kh, bh = ld("at"), ld("rt"), ld("kt"), ld("bt"), ld("kh"), ld("bh")
    gate = each(lambda i: g_scr[rs[i], cs[i]])
    bonus_v = each(lambda i: bv_scr[rs[i], cs[i]])
    p_last = each(lambda i: pl_scr[items[i][0], :, cs[i]])

    v_exp = each(lambda i: expand(vh[i]))
    ar = each(lambda i: jnp.concatenate([at[i], rt[i]], axis=0))
    mk = each(lambda i: mm_nt(ar[i], expand(kt[i])))
    mb = each(lambda i: mm_nt(ar[i], expand(bt[i])))
    m_ab = each(lambda i: jnp.where(strict, mb[i][:L], 0.0))
    m_kk = each(lambda i: jnp.concatenate([jnp.where(strict, mk[i][:L], 0.0),
                                           jnp.where(incl, mk[i][L:], 0.0)], axis=0))
    m_rb = each(lambda i: jnp.where(incl, mb[i][L:], 0.0))
    interleave()

    d1 = each(lambda i: jnp.where(blk16, m_ab[i], 0.0))
    e1 = each(lambda i: jnp.where(blk32 & jnp.logical_not(blk16), m_ab[i], 0.0))
    e2 = each(lambda i: jnp.where(jnp.logical_not(blk32), m_ab[i], 0.0))
    x = each(lambda i: eye + d1[i])
    d2 = each(lambda i: mm(d1[i], expand(d1[i])))
    ars = each(lambda i: mm_nt(ar[i], expand(s0[i])))
    kv = each(lambda i: mm(m_kk[i], v_exp[i]))
    interleave()
    x = each(lambda i: x[i] + mm(d2[i], expand(x[i])))
    d4 = each(lambda i: mm(d2[i], expand(d2[i])))
    interleave()
    x = each(lambda i: x[i] + mm(d4[i], expand(x[i])))
    d8 = each(lambda i: mm(d4[i], expand(d4[i])))
    interleave()
    x = each(lambda i: x[i] + mm(d8[i], expand(x[i])))
    t = each(lambda i: mm(e1[i], expand(x[i])))
    interleave()
    x = each(lambda i: x[i] + mm(x[i], expand(t[i])))
    t = each(lambda i: mm(e2[i], expand(x[i])))
    interleave()
    x = each(lambda i: x[i] + mm(x[i], expand(t[i])))
    u = each(lambda i: mm(x[i], expand(ars[i][:L] + kv[i][:L])))
    o = each(lambda i: ars[i][L:] + kv[i][L:] + mm(m_rb[i], expand(u[i])))
    s_add = each(lambda i: mm_tn(jnp.concatenate([vh[i], u[i].astype(BF16)], axis=0),
                                 jnp.concatenate([kh[i], bh[i]], axis=0)))
    for i in n:
        s_scr[i] = s0[i] * p_last[i] + s_add[i]

    seg = seg_ref[0:PW, 0:PW]
    inv_n = 1.0 / HEAD_DIM
    for i in n:
        mean = _dot1(o[i], seg) * inv_n
        cen = o[i] - mean
        var = _dot1(cen * cen, seg) * inv_n
        on = cen * lax.rsqrt(var + GN_EPS) * lng_ref[:, cs[i]] + lnb_ref[:, cs[i]]
        y_ref[items[i][0], :, cs[i]] = (on + bonus_v[i]) * gate[i]
    while pieces:
        interleave()


def _rwkv(proj, B, T, mu, w0, wup_pad, a0, aup_pad, gup, k_k, k_a, r_k, lng, lnb, seg):
    L = CHUNK
    nc = T // L
    nseq = RWKV_SEQS if B % RWKV_SEQS == 0 else 1
    R = nseq * L
    proj3 = proj.reshape(B, T, N_MIX)
    vec = lambda n: pl.BlockSpec((1, n), lambda b, s: (0, 0))
    mat = lambda m, n: pl.BlockSpec((m, n), lambda b, s: (0, 0))
    cur = lambda s: jnp.minimum(s, nc - 1)
    y = pl.pallas_call(
        _rwkv_kernel,
        grid=(B // nseq, nc + 1),
        in_specs=[
            pl.BlockSpec((nseq, L, RWKV_IN), lambda b, s: (b, cur(s), 0)),
            pl.BlockSpec((nseq, 8, RWKV_IN), lambda b, s: (b, jnp.maximum(cur(s) * (L // 8) - 1, 0), 0)),
            vec(RWKV_IN), vec(WIDTH), mat(128, WIDTH), vec(WIDTH), mat(128, WIDTH), mat(GATE_LORA, WIDTH),
            vec(WIDTH), vec(WIDTH), vec(WIDTH), vec(WIDTH), vec(WIDTH), mat(WIDTH, WIDTH),
        ],
        out_specs=pl.BlockSpec((nseq, L, WIDTH), lambda b, s: (b, jnp.maximum(s - 1, 0), 0)),
        out_shape=jax.ShapeDtypeStruct((B, T, WIDTH), F32),
        scratch_shapes=[pltpu.VMEM((nseq * RWKV_HEADS // PACK, HEAD_DIM, PACK * HEAD_DIM), F32)]
        + [pltpu.VMEM((R, WIDTH), BF16)] * len(RWKV_STAGED)
        + [pltpu.VMEM((R, WIDTH), F32), pltpu.VMEM((R, WIDTH), F32), pltpu.VMEM((nseq, 1, WIDTH), F32)],
        compiler_params=_params(("parallel", "arbitrary")),
        name="rwkv",
    )(proj3, proj3, mu, w0, wup_pad, a0, aup_pad, gup, k_k, k_a, r_k, lng, lnb, seg)
    return y.reshape(B * T, WIDTH)


def _bf16_split3(c):
    out, r = [], float(c)
    for _ in range(3):
        h = float(np.asarray(r, np.float32).astype(BF16).astype(np.float32))
        out.append(h)
        r -= h
    return out


LOG2E = float(np.log2(np.e))
Q_ROWS = 128
BIAS_ROW = HEAD_DIM
PEN_ROW = HEAD_DIM + 8
V_ROWS = HEAD_DIM + 16


def _attn_prep_kernel(q_ref, k_ref, v_ref, qg_ref, kg_ref, seg_ref, place_ref,
                      qt_out, k_out, vt_out, kmean_scr):
    i = pl.program_id(1)
    nbp = kmean_scr.shape[0]
    BLK = MOBA_BLOCK

    @pl.when(i == 0)
    def _():
        kmean_scr[...] = jnp.zeros_like(kmean_scr)

    inv_n = 1.0 / HEAD_DIM
    q = q_ref[...]
    k = k_ref[...]
    qn = q * lax.rsqrt(_dot1(q * q, seg_ref[...]) * inv_n + RMS_EPS) * qg_ref[...]
    kn = k * lax.rsqrt(_dot1(k * k, seg_ref[...]) * inv_n + RMS_EPS) * kg_ref[...]
    qt = (qn * (LOG2E * HEAD_DIM ** -0.5)).T

    kmean = jnp.mean(kn, axis=0, keepdims=True)
    rown = lax.broadcasted_iota(jnp.int32, kmean_scr.shape, 0)
    kmeans = jnp.where(rown == i, kmean, kmean_scr[...])
    kmean_scr[...] = kmeans

    kaug = _dot(kn.astype(BF16), place_ref[...])
    lane = lax.broadcasted_iota(jnp.int32, kaug.shape, 1) & 127
    pos = lax.broadcasted_iota(jnp.int32, kaug.shape, 0).astype(F32)
    kaug = jnp.where((lane >= BIAS_ROW) & (lane < BIAS_ROW + 3), pos, kaug)
    kaug = jnp.where((lane >= BIAS_ROW + 3) & (lane < BIAS_ROW + 6), i.astype(F32), kaug)
    kaug = jnp.where(lane == PEN_ROW + i, 1.0, kaug).astype(BF16)

    vt = v_ref[...].T
    sub16 = lax.broadcasted_iota(jnp.int32, (V_ROWS - HEAD_DIM, BLK), 0)
    ones_rows = jnp.where(sub16 == 0, 1.0, 0.0)

    blk_id = lax.broadcasted_iota(jnp.int32, (nbp, BLK), 0)
    sub8 = lax.broadcasted_iota(jnp.int32, (8, BLK), 0)
    tail = jnp.zeros((Q_ROWS - PEN_ROW - nbp, BLK), F32)
    for h in range(ATTN_HEADS):
        sl = slice(h * HEAD_DIM, (h + 1) * HEAD_DIM)
        qth = qt[sl, :]

        gate = _dot3(kmeans[:, sl], qth)
        gate = jnp.where(blk_id < i, gate, NEG_INF)
        chosen = blk_id == i
        for _ in range(MOBA_TOP_K):
            mx = jnp.max(gate, axis=0, keepdims=True)
            idx = jnp.min(jnp.where(gate == mx, blk_id, nbp), axis=0, keepdims=True)
            hit = blk_id == idx
            chosen = chosen | (hit & (blk_id < i))
            gate = jnp.where(hit, -jnp.inf, gate)
        penalty = jnp.where(chosen, 0.0, NEG_INF)

        slope = 2.0 ** (-8.0 * (h + 1) / ATTN_HEADS)
        cvals = _bf16_split3(slope * LOG2E) + _bf16_split3(slope * LOG2E * MOBA_BLOCK)
        consts = jnp.zeros((8, BLK), F32)
        for r, cv in enumerate(cvals):
            consts = jnp.where(sub8 == r, cv, consts)
        qt_out[0, h, 0] = jnp.concatenate([qth, consts, penalty, tail], axis=0).astype(BF16)
        k_out[0, h, 0] = kaug[:, h * 128:(h + 1) * 128]
        vt_out[0, h, 0] = jnp.concatenate([vt[sl, :], ones_rows], axis=0).astype(BF16)


def _attn_prep(proj, B, T, qg, kg, seg, place):
    nb = T // MOBA_BLOCK
    nbp = -(-nb // 8) * 8
    assert PEN_ROW + nbp <= Q_ROWS
    H = ATTN_HEADS
    col = lambda c: pl.BlockSpec((MOBA_BLOCK, WIDTH), lambda b, i: (b * nb + i, c))
    const = lambda m, n: pl.BlockSpec((m, n), lambda b, i: (0, 0))
    blk = lambda r, c: pl.BlockSpec((1, H, 1, r, c), lambda b, i: (b, 0, i, 0, 0))
    return pl.pallas_call(
        _attn_prep_kernel,
        grid=(B, nb),
        in_specs=[col(ATTN_COL), col(ATTN_COL + 1), col(ATTN_COL + 2),
                  const(1, WIDTH), const(1, WIDTH), const(WIDTH, WIDTH), const(WIDTH, 2 * WIDTH)],
        out_specs=[blk(Q_ROWS, MOBA_BLOCK), blk(MOBA_BLOCK, 128), blk(V_ROWS, MOBA_BLOCK)],
        out_shape=[
            jax.ShapeDtypeStruct((B, H, nb, Q_ROWS, MOBA_BLOCK), BF16),
            jax.ShapeDtypeStruct((B, H, nb, MOBA_BLOCK, 128), BF16),
            jax.ShapeDtypeStruct((B, H, nb, V_ROWS, MOBA_BLOCK), BF16),
        ],
        scratch_shapes=[pltpu.VMEM((nbp, WIDTH), F32)],
        compiler_params=_params(("parallel", "arbitrary")),
        name="attn_prep",
    )(proj, proj, proj, qg, kg, seg, place)


def _attn_kernel(q_ref, k_ref, v_ref, o_ref, sa_scr, sb_scr, ma_scr, mb_scr, m_scr, acc_scr):
    i = pl.program_id(1)
    BLK = MOBA_BLOCK
    rowk = lax.broadcasted_iota(jnp.int32, (BLK, BLK), 0)
    colq = lax.broadcasted_iota(jnp.int32, (BLK, BLK), 1)
    causal = rowk <= colq

    heads = range(ATTN_HEADS)
    q = [q_ref[0, h, 0] for h in heads]

    def scores_into(dst, h, j):
        s_h = _dot(k_ref[0, h, j], q[h])
        dst[0][h] = s_h
        dst[1][h] = jnp.max(s_h, axis=0, keepdims=True)

    def softmax_update(h, src, j):
        m_old = m_scr[h]
        m_new = jnp.maximum(m_old, src[1][h])
        alpha = jnp.exp2(m_old - m_new)
        p = jnp.exp2(src[0][h] - m_new).astype(BF16)
        m_scr[h] = m_new
        acc_scr[h] = acc_scr[h] * alpha + _dot(v_ref[0, h, j], p)

    def fused_step(dst, j_next, src, j):
        for h in heads:
            scores_into(dst, h, j_next)
            softmax_update(h, src, j)

    buf_a = (sa_scr, ma_scr)
    buf_b = (sb_scr, mb_scr)

    own = lambda h: _dot(k_ref[0, h, i], q[h])
    s_next = own(0)
    for h in heads:
        s_h = jnp.where(causal, s_next, NEG_INF)
        if h + 1 < ATTN_HEADS:
            s_next = own(h + 1)
        scores_into(buf_a, h, 0)
        m0 = jnp.max(s_h, axis=0, keepdims=True)
        m_scr[h] = m0
        acc_scr[h] = _dot(v_ref[0, h, i], jnp.exp2(s_h - m0).astype(BF16))

    def two_blocks(jj, carry):
        j = 2 * jj
        fused_step(buf_b, j + 1, buf_a, j)
        fused_step(buf_a, jnp.minimum(j + 2, i - 1), buf_b, j + 1)
        return carry

    lax.fori_loop(0, i // 2, two_blocks, 0)

    @pl.when(i % 2 == 1)
    def _():
        for h in heads:
            softmax_update(h, buf_a, i - 1)

    for h in heads:
        acc = acc_scr[h]
        o_ref[0, h * HEAD_DIM:(h + 1) * HEAD_DIM, :] = acc[:HEAD_DIM] / acc[HEAD_DIM:HEAD_DIM + 1]


def _attention(qt, kaug, vt, B, T):
    nb = T // MOBA_BLOCK
    H = ATTN_HEADS
    whole = lambda r, c: pl.BlockSpec((1, H, nb, r, c), lambda b, i: (b, 0, 0, 0, 0),
                                      pipeline_mode=pl.Buffered(1))
    tile = lambda r: pltpu.VMEM((H, r, MOBA_BLOCK), F32)
    return pl.pallas_call(
        _attn_kernel,
        grid=(B, nb),
        in_specs=[
            pl.BlockSpec((1, H, 1, Q_ROWS, MOBA_BLOCK), lambda b, i: (b, 0, i, 0, 0)),
            whole(MOBA_BLOCK, 128),
            whole(V_ROWS, MOBA_BLOCK),
        ],
        out_specs=pl.BlockSpec((1, WIDTH, MOBA_BLOCK), lambda b, i: (b, 0, i)),
        out_shape=jax.ShapeDtypeStruct((B, WIDTH, T), F32),
        scratch_shapes=[tile(MOBA_BLOCK), tile(MOBA_BLOCK), tile(1), tile(1), tile(1), tile(V_ROWS)],
        compiler_params=_params(("parallel", "arbitrary")),
        name="moba_attention",
    )(qt, kaug, vt)


POOL_HALO = 16


def _merge_kernel(x_ref, yr_ref, yat_ref, pp_ref, pprev_ref, g0_ref, g1_ref, g2_ref,
                  pw_ref, pscale_ref, wb_ref, wo_ref, o_ref):
    t = pl.program_id(1)
    tl = x_ref.shape[0]

    u = pp_ref[...]
    halo = jnp.where(t == 0, 0.0, pprev_ref[...])
    ext = jnp.concatenate([halo, u], axis=0)
    pos = (t * tl + lax.broadcasted_iota(jnp.int32, (tl, POOL_GROUP_DIM), 0) + 1).astype(F32)
    y_pool = []
    for gi, w in enumerate(POOL_WINDOWS):
        sl = slice(gi * POOL_GROUP_DIM, (gi + 1) * POOL_GROUP_DIM)
        acc = ext[:, sl]
        s = 1
        while s < w:
            acc = acc + pltpu.roll(acc, s, axis=0)
            s *= 2
        d = acc[POOL_HALO:] / jnp.minimum(pos, float(w)) - u[:, sl]
        y_pool.append(_dot(d.astype(BF16), pw_ref[gi]))
    y_pool = jnp.concatenate(y_pool, axis=1) * pscale_ref[...]

    br_r = _dot(yr_ref[...].astype(BF16), wb_ref[0])
    br_a = _dot(yat_ref[0].astype(BF16), wb_ref[1], TN)
    br_p = _dot(y_pool.astype(BF16), wb_ref[2])
    gate = lambda ref: jax.nn.sigmoid(ref[...].astype(F32))
    merged = gate(g0_ref) * br_r + gate(g1_ref) * br_a + gate(g2_ref) * br_p
    o_ref[...] = x_ref[...] + _dot(merged.astype(BF16), wo_ref[...])


def _merge(x2, y_rwkv, y_attn_t, proj, gates, B, T, pool_w, pool_scale, w_branch, w_out, tl=512):
    tl = min(tl, T)
    nt = T // tl
    rows = lambda w, c: pl.BlockSpec((tl, w), lambda b, t: (b * nt + t, c))
    const2 = lambda m, n: pl.BlockSpec((m, n), lambda b, t: (0, 0))
    const3 = lambda a, m, n: pl.BlockSpec((a, m, n), lambda b, t: (0, 0, 0))
    return pl.pallas_call(
        _merge_kernel,
        grid=(B, nt),
        in_specs=[
            rows(D_MODEL, 0),
            rows(WIDTH, 0),
            pl.BlockSpec((1, WIDTH, tl), lambda b, t: (b, 0, t)),
            rows(WIDTH, POOL_COL),
            pl.BlockSpec((POOL_HALO, WIDTH),
                         lambda b, t: (jnp.maximum((b * nt + t) * (tl // POOL_HALO) - 1, 0), POOL_COL)),
            rows(D_MODEL, 0), rows(D_MODEL, 1), rows(D_MODEL, 2),
            const3(len(POOL_WINDOWS), POOL_GROUP_DIM, POOL_GROUP_DIM),
            const2(1, WIDTH),
            const3(N_BRANCH, WIDTH, D_MODEL),
            const2(D_MODEL, D_MODEL),
        ],
        out_specs=rows(D_MODEL, 0),
        out_shape=jax.ShapeDtypeStruct((B * T, D_MODEL), F32),
        compiler_params=_params(("parallel", "arbitrary")),
        name="merge",
    )(x2, y_rwkv, y_attn_t, proj, proj, gates, gates, gates, pool_w, pool_scale, w_branch, w_out)


def _ffn_kernel(x_ref, g_ref, w1_ref, w2_ref, o_ref):
    x = x_ref[...]
    ms = jnp.mean(x * x, axis=-1, keepdims=True)
    h = (x * lax.rsqrt(ms + RMS_EPS) * g_ref[...]).astype(BF16)
    acc = x
    tf = D_MODEL
    for f in range(D_FF // tf):
        a = jnp.maximum(_dot(h, w1_ref[:, f * tf:(f + 1) * tf]), 0.0)
        acc = acc + _dot((a * a).astype(BF16), w2_ref[f * tf:(f + 1) * tf, :])
    o_ref[...] = acc


def _ffn(x2, g, w1, w2, tm=512):
    m = x2.shape[0]
    tm = min(tm, m)
    resident = lambda r, c: pl.BlockSpec((r, c), lambda i: (0, 0), pipeline_mode=pl.Buffered(1))
    return pl.pallas_call(
        _ffn_kernel,
        grid=(m // tm,),
        in_specs=[
            pl.BlockSpec((tm, D_MODEL), lambda i: (i, 0)),
            pl.BlockSpec((1, D_MODEL), lambda i: (0, 0)),
            resident(D_MODEL, D_FF),
            resident(D_FF, D_MODEL),
        ],
        out_specs=pl.BlockSpec((tm, D_MODEL), lambda i: (i, 0)),
        out_shape=jax.ShapeDtypeStruct((m, D_MODEL), F32),
        compiler_params=_params(("parallel",)),
        name="ffn",
    )(x2, g, w1, w2)


def _segment_ones():
    idx = np.arange(WIDTH) // HEAD_DIM
    return jnp.asarray(idx[:, None] == idx[None, :], BF16)


def _head_placement():
    src = np.arange(WIDTH)
    dst = (src // HEAD_DIM) * 128 + src % HEAD_DIM
    m = np.zeros((WIDTH, 2 * WIDTH), np.float32)
    m[src, dst] = 1.0
    return jnp.asarray(m, BF16)


def kernel(x, norm1_g, w_in, shift_mu, w0, w_up, a0, a_up, g_up, k_k, k_a, r_k, lnx_g, lnx_b,
           qn_g, kn_g, pool_w, pool_scale, w_branch, w_out, norm2_g, w_ff1, w_ff2):
    B, T, D = x.shape
    depth = w_in.shape[0]
    assert D == D_MODEL and T % MOBA_BLOCK == 0
    out_dtype = x.dtype
    x2 = x.astype(F32).reshape(B * T, D)
    seg = _segment_ones()
    place = _head_placement()
    row = lambda a: a.reshape(1, -1).astype(F32)
    zeros_lora = jnp.zeros((64, WIDTH), F32)

    for l in range(depth):
        w = w_in[l]
        w_mix = jnp.concatenate([w[:, :RWKV_IN], jnp.zeros((D, RWKV_PAD - RWKV_IN), w.dtype),
                                 w[:, RWKV_IN:RWKV_IN + 4 * WIDTH]], axis=1).astype(BF16)
        proj, gates = _inproj(x2, row(norm1_g[l]), w_mix, w[:, RWKV_IN + 4 * WIDTH:].astype(BF16))

        wup_pad = jnp.concatenate([w_up[l], zeros_lora], axis=0)
        aup_pad = jnp.concatenate([zeros_lora, a_up[l]], axis=0)
        y_rwkv = _rwkv(proj, B, T, row(shift_mu[l]), row(w0[l]), wup_pad, row(a0[l]), aup_pad, g_up[l],
                       row(k_k[l]), row(k_a[l]), row(r_k[l]), row(lnx_g[l]), row(lnx_b[l]), seg)

        qt, kaug, vt = _attn_prep(proj, B, T, row(jnp.tile(qn_g[l], ATTN_HEADS)),
                                  row(jnp.tile(kn_g[l], ATTN_HEADS)), seg, place)
        y_attn_t = _attention(qt, kaug, vt, B, T)

        x2 = _merge(x2, y_rwkv, y_attn_t, proj, gates, B, T, pool_w[l].astype(BF16), row(pool_scale[l]),
                    w_branch[l].astype(BF16), w_out[l].astype(BF16))
        x2 = _ffn(x2, row(norm2_g[l]), w_ff1[l].astype(BF16), w_ff2[l].astype(BF16))

    return x2.reshape(B, T, D).astype(out_dtype)
```

```python
import numpy as np
import jax
import jax.numpy as jnp
from jax import lax
from jax.experimental import pallas as pl
from jax.experimental.pallas import tpu as pltpu

F32 = jnp.float32
BF16 = jnp.bfloat16

D_MODEL = 1024
RWKV_HEADS = 8
HEAD_DIM = 64
WIDTH = RWKV_HEADS * HEAD_DIM
DECAY_LORA = 64
ICLR_LORA = 64
GATE_LORA = 128
ATTN_HEADS = 8
MOBA_BLOCK = 256
MOBA_TOP_K = 3
POOL_WINDOWS = (2, 4, 8, 16)
POOL_GROUP_DIM = 128
N_BRANCH = 3
D_FF = 4 * D_MODEL
RMS_EPS = 1e-6
GN_EPS = 64e-5
NEG_INF = -1e30

RWKV_IN = 3 * WIDTH + DECAY_LORA + ICLR_LORA + GATE_LORA
RWKV_PAD = 2048
ATTN_COL = RWKV_PAD // WIDTH
POOL_COL = ATTN_COL + 3
N_MIX = RWKV_PAD + 4 * WIDTH
N_GATE = N_BRANCH * D_MODEL

CHUNK = 64
RWKV_SEQS = 4
PACK = 4
VMEM_LIMIT = 56 * 1024 * 1024

NN = (((1,), (0,)), ((), ()))
NT = (((1,), (1,)), ((), ()))
TN = (((0,), (0,)), ((), ()))


def _dot(a, b, dims=NN):
    return lax.dot_general(a, b, dims, preferred_element_type=F32)


def _dot1(a, b, dims=NN):
    return _dot(a.astype(BF16), b.astype(BF16), dims)


def _split(a):
    hi = a.astype(BF16)
    lo = (a - hi.astype(F32)).astype(BF16)
    return hi, lo


def _dot3(a, b, dims=NN):
    ah, al = _split(a)
    bh, bl = _split(b)
    return _dot(ah, bh, dims) + (_dot(ah, bl, dims) + _dot(al, bh, dims))


def _params(sem):
    return pltpu.CompilerParams(dimension_semantics=sem, vmem_limit_bytes=VMEM_LIMIT)


def _inproj_kernel(x_ref, g_ref, wm_ref, wg_ref, om_ref, og_ref):
    x = x_ref[...]
    ms = jnp.mean(x * x, axis=-1, keepdims=True)
    h = (x * lax.rsqrt(ms + RMS_EPS) * g_ref[...]).astype(BF16)
    tn = D_MODEL
    for n in range(N_MIX // tn):
        om_ref[:, n * tn:(n + 1) * tn] = _dot(h, wm_ref[:, n * tn:(n + 1) * tn])
    for n in range(N_GATE // tn):
        og_ref[:, n * tn:(n + 1) * tn] = _dot(h, wg_ref[:, n * tn:(n + 1) * tn]).astype(BF16)


def _inproj(x2, g, w_mix, w_gate, tm=512):
    m = x2.shape[0]
    tm = min(tm, m)
    resident = lambda n: pl.BlockSpec((D_MODEL, n), lambda i: (0, 0), pipeline_mode=pl.Buffered(1))
    return pl.pallas_call(
        _inproj_kernel,
        grid=(m // tm,),
        in_specs=[
            pl.BlockSpec((tm, D_MODEL), lambda i: (i, 0)),
            pl.BlockSpec((1, D_MODEL), lambda i: (0, 0)),
            resident(N_MIX),
            resident(N_GATE),
        ],
        out_specs=[pl.BlockSpec((tm, N_MIX), lambda i: (i, 0)), pl.BlockSpec((tm, N_GATE), lambda i: (i, 0))],
        out_shape=[jax.ShapeDtypeStruct((m, N_MIX), F32), jax.ShapeDtypeStruct((m, N_GATE), BF16)],
        compiler_params=_params(("parallel",)),
        name="inproj",
    )(x2, g, w_mix, w_gate)


RWKV_STAGED = ("at", "rt", "kt", "bt", "kh", "bh", "v")


def _rwkv_kernel(p_ref, prev_ref, mu_ref, w0_ref, wup_ref, a0_ref, aup_ref, gup_ref,
                 kk_ref, ka_ref, rk_ref, lng_ref, lnb_ref, seg_ref, y_ref,
                 s_scr, at_scr, rt_scr, kt_scr, bt_scr, kh_scr, bh_scr, v_scr, g_scr, bv_scr, pl_scr):
    step = pl.program_id(1)
    nc = pl.num_programs(1) - 1
    L = CHUNK
    W = WIDTH
    nseq = p_ref.shape[0]
    R = nseq * L
    staged = dict(zip(RWKV_STAGED, (at_scr, rt_scr, kt_scr, bt_scr, kh_scr, bh_scr, v_scr)))

    @pl.when(step == 0)
    def _():
        for ref in (s_scr, g_scr, bv_scr, pl_scr) + tuple(staged.values()):
            ref[...] = jnp.zeros_like(ref)

    PW = PACK * HEAD_DIM
    rowp = lax.broadcasted_iota(jnp.int32, (L, PW), 0)
    lanep = lax.broadcasted_iota(jnp.int32, (L, PW), 1)
    colp = lanep & (HEAD_DIM - 1)
    head_of_lane = lanep >> 6
    strict = rowp > colp
    incl = rowp >= colp
    eye = (rowp == colp).astype(F32)
    blk16 = (rowp >> 4) == (colp >> 4)
    blk32 = (rowp >> 5) == (colp >> 5)

    half_of_lane = lax.broadcasted_iota(jnp.int32, (L, 128), 1) >> 6
    zero_tile = jnp.zeros((L, 128), BF16)

    def expand(x):
        xb = x.astype(BF16)
        copies = []
        for a in range(PACK):
            kept = jnp.where(half_of_lane == (a % 2), xb[:, (a // 2) * 128:(a // 2 + 1) * 128], zero_tile)
            copies.append(jnp.concatenate([kept, zero_tile] if a < 2 else [zero_tile, kept], axis=1))
        return jnp.concatenate(copies, axis=0)

    def mm(a, x_exp):
        return _dot(a.astype(BF16), x_exp)

    def mm_nt(a, b_exp):
        return _dot(a.astype(BF16), b_exp, NT)

    def mm_tn(a, b):
        full = _dot(a.astype(BF16), b.astype(BF16), TN)
        out = jnp.where(head_of_lane == 0, full[0:HEAD_DIM], 0.0)
        for hh in range(1, PACK):
            out = out + jnp.where(head_of_lane == hh, full[hh * HEAD_DIM:(hh + 1) * HEAD_DIM], 0.0)
        return out

    first_chunk = jnp.minimum(step, nc - 1) == 0

    def mixed(c0, c1):
        out = []
        for b in range(nseq):
            pc = p_ref[b, :, c0:c1]
            first = jnp.where(first_chunk, 0.0, prev_ref[b, 7:8, c0:c1])
            shifted = jnp.where(rows_like(pc) == 0, first, pltpu.roll(pc, 1, axis=0))
            out.append(pc + (shifted - pc) * mu_ref[:, c0:c1])
        return jnp.concatenate(out, axis=0)

    def rows_like(x):
        return lax.broadcasted_iota(jnp.int32, x.shape, 0)

    def seg_sum(z):
        seg_pw = seg_ref[0:PW, 0:PW]
        return jnp.concatenate([_dot1(z[:, c:c + PW], seg_pw) for c in range(0, W, PW)], axis=1)

    pre = {}

    def prep_decay():
        lo = mixed(3 * W, 3 * W + 128)
        pre["lo"] = lo
        wpre = w0_ref[...] + _dot3(jnp.tanh(lo), wup_ref[...])
        pre["logw"] = -float(np.exp(-0.5)) * jax.nn.sigmoid(wpre)

    def prep_gates():
        xg = mixed(3 * W + 128, 3 * W + 256)
        pre["a"] = jax.nn.sigmoid(a0_ref[...] + _dot1(pre["lo"], aup_ref[...]))
        g_scr[...] = _dot1(jax.nn.sigmoid(xg), gup_ref[...])

    def prep_cumsum():
        logw = pre["logw"]
        tl = rows_like(logw) & (L - 1)
        cl = logw
        sh = 1
        while sh < L:
            cl = cl + jnp.where(tl >= sh, pltpu.roll(cl, sh, axis=0), 0.0)
            sh *= 2
        pre["cl"] = cl

    def prep_decay_factors():
        cl = pre["cl"]
        cl_last = [cl[(b + 1) * L - 1:(b + 1) * L, :] for b in range(nseq)]
        cl_end = cl_last[nseq - 1]
        rows = rows_like(cl)
        for b in range(nseq - 2, -1, -1):
            cl_end = jnp.where(rows < (b + 1) * L, cl_last[b], cl_end)
        pre["e_neg"] = jnp.exp(-cl)
        pre["e_rem"] = jnp.exp(cl_end - cl)
        for b in range(nseq):
            pl_scr[b] = jnp.exp(cl_last[b])

    def prep_k():
        k = mixed(W, 2 * W)
        kk = k * kk_ref[...]
        ss = seg_sum(kk * kk)
        pre["kk"] = kk * lax.rsqrt(jnp.maximum(ss, 1e-24))
        pre["k2"] = k * (1.0 + (pre["a"] - 1.0) * ka_ref[...])

    def prep_ab():
        bvec = pre["kk"] * pre["a"]
        at_scr[...] = (-pre["kk"] * jnp.exp(pre["cl"] - pre["logw"])).astype(BF16)
        bt_scr[...] = (bvec * pre["e_neg"]).astype(BF16)
        bh_scr[...] = (bvec * pre["e_rem"]).astype(BF16)

    def prep_kk():
        kt_scr[...] = (pre["k2"] * pre["e_neg"]).astype(BF16)
        kh_scr[...] = (pre["k2"] * pre["e_rem"]).astype(BF16)

    def prep_r():
        r = mixed(0, W)
        rt_scr[...] = (r * jnp.exp(pre["cl"])).astype(BF16)
        pre["rk"] = r * pre["k2"] * rk_ref[...]

    def prep_v():
        v = mixed(2 * W, 3 * W)
        v_scr[...] = v.astype(BF16)
        bv_scr[...] = seg_sum(pre["rk"]) * v

    pieces = [prep_decay, prep_gates, prep_cumsum, prep_decay_factors, prep_k, prep_ab, prep_kk, prep_r, prep_v]

    def interleave():
        if pieces:
            pieces.pop(0)()

    items = [(b, gq) for b in range(nseq) for gq in range(RWKV_HEADS // PACK)]
    n = range(len(items))
    rs = [slice(b * L, (b + 1) * L) for b, _ in items]
    cs = [slice(gq * PW, (gq + 1) * PW) for _, gq in items]
    each = lambda f: [f(i) for i in n]
    ld = lambda name: each(lambda i: staged[name][rs[i], cs[i]])

    s0 = each(lambda i: jnp.where(step <= 1, 0.0, s_scr[i]))
    vh = ld("v")
    at, rt, kt, bt, kh, bh = ld("at"), ld("rt"), ld("kt"), ld("bt"), ld("kh"), ld("bh")
    gate = each(lambda i: g_scr[rs[i], cs[i]])
    bonus_v = each(lambda i: bv_scr[rs[i], cs[i]])
    p_last = each(lambda i: pl_scr[items[i][0], :, cs[i]])

    v_exp = each(lambda i: expand(vh[i]))
    ar = each(lambda i: jnp.concatenate([at[i], rt[i]], axis=0))
    mk = each(lambda i: mm_nt(ar[i], expand(kt[i])))
    mb = each(lambda i: mm_nt(ar[i], expand(bt[i])))
    m_ab = each(lambda i: jnp.where(strict, mb[i][:L], 0.0))
    m_kk = each(lambda i: jnp.concatenate([jnp.where(strict, mk[i][:L], 0.0),
                                           jnp.where(incl, mk[i][L:], 0.0)], axis=0))
    m_rb = each(lambda i: jnp.where(incl, mb[i][L:], 0.0))
    interleave()

    d1 = each(lambda i: jnp.where(blk16, m_ab[i], 0.0))
    e1 = each(lambda i: jnp.where(blk32 & jnp.logical_not(blk16), m_ab[i], 0.0))
    e2 = each(lambda i: jnp.where(jnp.logical_not(blk32), m_ab[i], 0.0))
    x = each(lambda i: eye + d1[i])
    d2 = each(lambda i: mm(d1[i], expand(d1[i])))
    ars = each(lambda i: mm_nt(ar[i], expand(s0[i])))
    kv = each(lambda i: mm(m_kk[i], v_exp[i]))
    interleave()
    x = each(lambda i: x[i] + mm(d2[i], expand(x[i])))
    d4 = each(lambda i: mm(d2[i], expand(d2[i])))
    interleave()
    x = each(lambda i: x[i] + mm(d4[i], expand(x[i])))
    d8 = each(lambda i: mm(d4[i], expand(d4[i])))
    interleave()
    x = each(lambda i: x[i] + mm(d8[i], expand(x[i])))
    t = each(lambda i: mm(e1[i], expand(x[i])))
    interleave()
    x = each(lambda i: x[i] + mm(x[i], expand(t[i])))
    t = each(lambda i: mm(e2[i], expand(x[i])))
    interleave()
    x = each(lambda i: x[i] + mm(x[i], expand(t[i])))
    interleave()
    u = each(lambda i: mm(x[i], expand(ars[i][:L] + kv[i][:L])))
    interleave()
    o = each(lambda i: ars[i][L:] + kv[i][L:] + mm(m_rb[i], expand(u[i])))
    s_add = each(lambda i: mm_tn(jnp.concatenate([vh[i], u[i].astype(BF16)], axis=0),
                                 jnp.concatenate([kh[i], bh[i]], axis=0)))
    interleave()
    for i in n:
        s_scr[i] = s0[i] * p_last[i] + s_add[i]

    seg = seg_ref[0:PW, 0:PW]
    inv_n = 1.0 / HEAD_DIM
    for i in n:
        mean = _dot1(o[i], seg) * inv_n
        cen = o[i] - mean
        var = _dot1(cen * cen, seg) * inv_n
        on = cen * lax.rsqrt(var + GN_EPS) * lng_ref[:, cs[i]] + lnb_ref[:, cs[i]]
        y_ref[items[i][0], :, cs[i]] = (on + bonus_v[i]) * gate[i]
    while pieces:
        interleave()


def _rwkv(proj, B, T, mu, w0, wup_pad, a0, aup_pad, gup, k_k, k_a, r_k, lng, lnb, seg):
    L = CHUNK
    nc = T // L
    nseq = RWKV_SEQS if B % RWKV_SEQS == 0 else 1
    R = nseq * L
    proj3 = proj.reshape(B, T, N_MIX)
    vec = lambda n: pl.BlockSpec((1, n), lambda b, s: (0, 0))
    mat = lambda m, n: pl.BlockSpec((m, n), lambda b, s: (0, 0))
    cur = lambda s: jnp.minimum(s, nc - 1)
    y = pl.pallas_call(
        _rwkv_kernel,
        grid=(B // nseq, nc + 1),
        in_specs=[
            pl.BlockSpec((nseq, L, RWKV_IN), lambda b, s: (b, cur(s), 0)),
            pl.BlockSpec((nseq, 8, RWKV_IN), lambda b, s: (b, jnp.maximum(cur(s) * (L // 8) - 1, 0), 0)),
            vec(RWKV_IN), vec(WIDTH), mat(128, WIDTH), vec(WIDTH), mat(128, WIDTH), mat(GATE_LORA, WIDTH),
            vec(WIDTH), vec(WIDTH), vec(WIDTH), vec(WIDTH), vec(WIDTH), mat(WIDTH, WIDTH),
        ],
        out_specs=pl.BlockSpec((nseq, L, WIDTH), lambda b, s: (b, jnp.maximum(s - 1, 0), 0)),
        out_shape=jax.ShapeDtypeStruct((B, T, WIDTH), F32),
        scratch_shapes=[pltpu.VMEM((nseq * RWKV_HEADS // PACK, HEAD_DIM, PACK * HEAD_DIM), F32)]
        + [pltpu.VMEM((R, WIDTH), BF16)] * len(RWKV_STAGED)
        + [pltpu.VMEM((R, WIDTH), F32), pltpu.VMEM((R, WIDTH), F32), pltpu.VMEM((nseq, 1, WIDTH), F32)],
        compiler_params=_params(("parallel", "arbitrary")),
        name="rwkv",
    )(proj3, proj3, mu, w0, wup_pad, a0, aup_pad, gup, k_k, k_a, r_k, lng, lnb, seg)
    return y.reshape(B * T, WIDTH)


def _bf16_split3(c):
    out, r = [], float(c)
    for _ in range(3):
        h = float(np.asarray(r, np.float32).astype(BF16).astype(np.float32))
        out.append(h)
        r -= h
    return out


LOG2E = float(np.log2(np.e))
Q_ROWS = 128
BIAS_ROW = HEAD_DIM
PEN_ROW = HEAD_DIM + 8
V_ROWS = HEAD_DIM + 16


def _attn_prep_kernel(q_ref, k_ref, v_ref, qg_ref, kg_ref, seg_ref, place_ref,
                      qt_out, k_out, vt_out, kmean_scr):
    i = pl.program_id(1)
    nbp = kmean_scr.shape[0]
    BLK = MOBA_BLOCK

    @pl.when(i == 0)
    def _():
        kmean_scr[...] = jnp.zeros_like(kmean_scr)

    inv_n = 1.0 / HEAD_DIM
    q = q_ref[...]
    k = k_ref[...]
    half = WIDTH // 2
    seg_half = seg_ref[0:half, 0:half]
    seg_sum = lambda z: jnp.concatenate([_dot1(z[:, :half], seg_half), _dot1(z[:, half:], seg_half)], axis=1)
    qn = q * lax.rsqrt(seg_sum(q * q) * inv_n + RMS_EPS) * qg_ref[...]
    kn = k * lax.rsqrt(seg_sum(k * k) * inv_n + RMS_EPS) * kg_ref[...]
    qt = (qn * (LOG2E * HEAD_DIM ** -0.5)).T

    kmean = jnp.mean(kn, axis=0, keepdims=True)
    rown = lax.broadcasted_iota(jnp.int32, kmean_scr.shape, 0)
    kmeans = jnp.where(rown == i, kmean, kmean_scr[...])
    kmean_scr[...] = kmeans

    kaug = _dot(kn.astype(BF16), place_ref[...])
    lane = lax.broadcasted_iota(jnp.int32, kaug.shape, 1) & 127
    pos = lax.broadcasted_iota(jnp.int32, kaug.shape, 0).astype(F32)
    kaug = jnp.where((lane >= BIAS_ROW) & (lane < BIAS_ROW + 3), pos, kaug)
    kaug = jnp.where((lane >= BIAS_ROW + 3) & (lane < BIAS_ROW + 6), i.astype(F32), kaug)
    kaug = jnp.where(lane == PEN_ROW + i, 1.0, kaug).astype(BF16)

    vt = v_ref[...].T
    sub16 = lax.broadcasted_iota(jnp.int32, (V_ROWS - HEAD_DIM, BLK), 0)
    ones_rows = jnp.where(sub16 == 0, 1.0, 0.0)

    blk_id = lax.broadcasted_iota(jnp.int32, (nbp, BLK), 0)
    sub8 = lax.broadcasted_iota(jnp.int32, (8, BLK), 0)
    tail = jnp.zeros((Q_ROWS - PEN_ROW - nbp, BLK), F32)
    for h in range(ATTN_HEADS):
        sl = slice(h * HEAD_DIM, (h + 1) * HEAD_DIM)
        qth = qt[sl, :]

        gate = _dot3(kmeans[:, sl], qth)
        gate = jnp.where(blk_id < i, gate, NEG_INF)
        chosen = blk_id == i
        for _ in range(MOBA_TOP_K):
            mx = jnp.max(gate, axis=0, keepdims=True)
            idx = jnp.min(jnp.where(gate == mx, blk_id, nbp), axis=0, keepdims=True)
            hit = blk_id == idx
            chosen = chosen | (hit & (blk_id < i))
            gate = jnp.where(hit, -jnp.inf, gate)
        penalty = jnp.where(chosen, 0.0, NEG_INF)

        slope = 2.0 ** (-8.0 * (h + 1) / ATTN_HEADS)
        cvals = _bf16_split3(slope * LOG2E) + _bf16_split3(slope * LOG2E * MOBA_BLOCK)
        consts = jnp.zeros((8, BLK), F32)
        for r, cv in enumerate(cvals):
            consts = jnp.where(sub8 == r, cv, consts)
        qt_out[0, h, 0] = jnp.concatenate([qth, consts, penalty, tail], axis=0).astype(BF16)
        k_out[0, h, 0] = kaug[:, h * 128:(h + 1) * 128]
        vt_out[0, h, 0] = jnp.concatenate([vt[sl, :], ones_rows], axis=0).astype(BF16)


def _attn_prep(proj, B, T, qg, kg, seg, place):
    nb = T // MOBA_BLOCK
    nbp = -(-nb // 8) * 8
    assert PEN_ROW + nbp <= Q_ROWS
    H = ATTN_HEADS
    col = lambda c: pl.BlockSpec((MOBA_BLOCK, WIDTH), lambda b, i: (b * nb + i, c))
    const = lambda m, n: pl.BlockSpec((m, n), lambda b, i: (0, 0))
    blk = lambda r, c: pl.BlockSpec((1, H, 1, r, c), lambda b, i: (b, 0, i, 0, 0))
    return pl.pallas_call(
        _attn_prep_kernel,
        grid=(B, nb),
        in_specs=[col(ATTN_COL), col(ATTN_COL + 1), col(ATTN_COL + 2),
                  const(1, WIDTH), const(1, WIDTH), const(WIDTH, WIDTH), const(WIDTH, 2 * WIDTH)],
        out_specs=[blk(Q_ROWS, MOBA_BLOCK), blk(MOBA_BLOCK, 128), blk(V_ROWS, MOBA_BLOCK)],
        out_shape=[
            jax.ShapeDtypeStruct((B, H, nb, Q_ROWS, MOBA_BLOCK), BF16),
            jax.ShapeDtypeStruct((B, H, nb, MOBA_BLOCK, 128), BF16),
            jax.ShapeDtypeStruct((B, H, nb, V_ROWS, MOBA_BLOCK), BF16),
        ],
        scratch_shapes=[pltpu.VMEM((nbp, WIDTH), F32)],
        compiler_params=_params(("parallel", "arbitrary")),
        name="attn_prep",
    )(proj, proj, proj, qg, kg, seg, place)


def _attn_kernel(q_ref, k_ref, v_ref, o_ref, sa_scr, sb_scr, ma_scr, mb_scr, m_scr, acc_scr):
    i = pl.program_id(1)
    BLK = MOBA_BLOCK
    rowk = lax.broadcasted_iota(jnp.int32, (BLK, BLK), 0)
    colq = lax.broadcasted_iota(jnp.int32, (BLK, BLK), 1)
    causal = rowk <= colq

    heads = range(ATTN_HEADS)
    q = [q_ref[0, h, 0] for h in heads]

    def scores_into(dst, h, j):
        s_h = _dot(k_ref[0, h, j], q[h])
        dst[0][h] = s_h
        dst[1][h] = jnp.max(s_h, axis=0, keepdims=True)

    def softmax_update(h, src, j):
        m_old = m_scr[h]
        m_new = jnp.maximum(m_old, src[1][h])
        alpha = jnp.exp2(m_old - m_new)
        p = jnp.exp2(src[0][h] - m_new).astype(BF16)
        m_scr[h] = m_new
        acc_scr[h] = acc_scr[h] * alpha + _dot(v_ref[0, h, j], p)

    def fused_step(dst, j_next, src, j):
        for h in heads:
            scores_into(dst, h, j_next)
            softmax_update(h, src, j)

    buf_a = (sa_scr, ma_scr)
    buf_b = (sb_scr, mb_scr)

    own = lambda h: _dot(k_ref[0, h, i], q[h])
    s_next = own(0)
    for h in heads:
        s_h = jnp.where(causal, s_next, NEG_INF)
        if h + 1 < ATTN_HEADS:
            s_next = own(h + 1)
        scores_into(buf_a, h, 0)
        m0 = jnp.max(s_h, axis=0, keepdims=True)
        m_scr[h] = m0
        acc_scr[h] = _dot(v_ref[0, h, i], jnp.exp2(s_h - m0).astype(BF16))

    def two_blocks(jj, carry):
        j = 2 * jj
        fused_step(buf_b, j + 1, buf_a, j)
        fused_step(buf_a, jnp.minimum(j + 2, i - 1), buf_b, j + 1)
        return carry

    lax.fori_loop(0, i // 2, two_blocks, 0)

    @pl.when(i % 2 == 1)
    def _():
        for h in heads:
            softmax_update(h, buf_a, i - 1)

    for h in heads:
        acc = acc_scr[h]
        o_ref[0, h * HEAD_DIM:(h + 1) * HEAD_DIM, :] = acc[:HEAD_DIM] / acc[HEAD_DIM:HEAD_DIM + 1]


def _attention(qt, kaug, vt, B, T):
    nb = T // MOBA_BLOCK
    H = ATTN_HEADS
    whole = lambda r, c: pl.BlockSpec((1, H, nb, r, c), lambda b, i: (b, 0, 0, 0, 0),
                                      pipeline_mode=pl.Buffered(1))
    tile = lambda r: pltpu.VMEM((H, r, MOBA_BLOCK), F32)
    return pl.pallas_call(
        _attn_kernel,
        grid=(B, nb),
        in_specs=[
            pl.BlockSpec((1, H, 1, Q_ROWS, MOBA_BLOCK), lambda b, i: (b, 0, i, 0, 0)),
            whole(MOBA_BLOCK, 128),
            whole(V_ROWS, MOBA_BLOCK),
        ],
        out_specs=pl.BlockSpec((1, WIDTH, MOBA_BLOCK), lambda b, i: (b, 0, i)),
        out_shape=jax.ShapeDtypeStruct((B, WIDTH, T), F32),
        scratch_shapes=[tile(MOBA_BLOCK), tile(MOBA_BLOCK), tile(1), tile(1), tile(1), tile(V_ROWS)],
        compiler_params=_params(("parallel", "arbitrary")),
        name="moba_attention",
    )(qt, kaug, vt)


POOL_HALO = 16


def _merge_kernel(x_ref, yr_ref, yat_ref, pp_ref, pprev_ref, g0_ref, g1_ref, g2_ref,
                  pw_ref, pscale_ref, wb_ref, wo_ref, o_ref):
    t = pl.program_id(1)
    tl = x_ref.shape[0]

    u = pp_ref[...]
    halo = jnp.where(t == 0, 0.0, pprev_ref[...])
    ext = jnp.concatenate([halo, u], axis=0)
    pos = (t * tl + lax.broadcasted_iota(jnp.int32, (tl, POOL_GROUP_DIM), 0) + 1).astype(F32)
    y_pool = []
    for gi, w in enumerate(POOL_WINDOWS):
        sl = slice(gi * POOL_GROUP_DIM, (gi + 1) * POOL_GROUP_DIM)
        acc = ext[:, sl]
        s = 1
        while s < w:
            acc = acc + pltpu.roll(acc, s, axis=0)
            s *= 2
        d = acc[POOL_HALO:] / jnp.minimum(pos, float(w)) - u[:, sl]
        y_pool.append(_dot(d.astype(BF16), pw_ref[gi]))
    y_pool = jnp.concatenate(y_pool, axis=1) * pscale_ref[...]

    br_r = _dot(yr_ref[...].astype(BF16), wb_ref[0])
    br_a = _dot(yat_ref[0].astype(BF16), wb_ref[1], TN)
    br_p = _dot(y_pool.astype(BF16), wb_ref[2])
    gate = lambda ref: jax.nn.sigmoid(ref[...].astype(F32))
    merged = gate(g0_ref) * br_r + gate(g1_ref) * br_a + gate(g2_ref) * br_p
    o_ref[...] = x_ref[...] + _dot(merged.astype(BF16), wo_ref[...])


def _merge(x2, y_rwkv, y_attn_t, proj, gates, B, T, pool_w, pool_scale, w_branch, w_out, tl=512):
    tl = min(tl, T)
    nt = T // tl
    rows = lambda w, c: pl.BlockSpec((tl, w), lambda b, t: (b * nt + t, c))
    const2 = lambda m, n: pl.BlockSpec((m, n), lambda b, t: (0, 0))
    const3 = lambda a, m, n: pl.BlockSpec((a, m, n), lambda b, t: (0, 0, 0))
    return pl.pallas_call(
        _merge_kernel,
        grid=(B, nt),
        in_specs=[
            rows(D_MODEL, 0),
            rows(WIDTH, 0),
            pl.BlockSpec((1, WIDTH, tl), lambda b, t: (b, 0, t)),
            rows(WIDTH, POOL_COL),
            pl.BlockSpec((POOL_HALO, WIDTH),
                         lambda b, t: (jnp.maximum((b * nt + t) * (tl // POOL_HALO) - 1, 0), POOL_COL)),
            rows(D_MODEL, 0), rows(D_MODEL, 1), rows(D_MODEL, 2),
            const3(len(POOL_WINDOWS), POOL_GROUP_DIM, POOL_GROUP_DIM),
            const2(1, WIDTH),
            const3(N_BRANCH, WIDTH, D_MODEL),
            const2(D_MODEL, D_MODEL),
        ],
        out_specs=rows(D_MODEL, 0),
        out_shape=jax.ShapeDtypeStruct((B * T, D_MODEL), F32),
        compiler_params=_params(("parallel", "arbitrary")),
        name="merge",
    )(x2, y_rwkv, y_attn_t, proj, proj, gates, gates, gates, pool_w, pool_scale, w_branch, w_out)


def _ffn_kernel(x_ref, g_ref, w1_ref, w2_ref, o_ref):
    x = x_ref[...]
    ms = jnp.mean(x * x, axis=-1, keepdims=True)
    h = (x * lax.rsqrt(ms + RMS_EPS) * g_ref[...]).astype(BF16)
    acc = x
    tf = D_MODEL
    for f in range(D_FF // tf):
        a = jnp.maximum(_dot(h, w1_ref[:, f * tf:(f + 1) * tf]), 0.0)
        acc = acc + _dot((a * a).astype(BF16), w2_ref[f * tf:(f + 1) * tf, :])
    o_ref[...] = acc


def _ffn(x2, g, w1, w2, tm=512):
    m = x2.shape[0]
    tm = min(tm, m)
    resident = lambda r, c: pl.BlockSpec((r, c), lambda i: (0, 0), pipeline_mode=pl.Buffered(1))
    return pl.pallas_call(
        _ffn_kernel,
        grid=(m // tm,),
        in_specs=[
            pl.BlockSpec((tm, D_MODEL), lambda i: (i, 0)),
            pl.BlockSpec((1, D_MODEL), lambda i: (0, 0)),
            resident(D_MODEL, D_FF),
            resident(D_FF, D_MODEL),
        ],
        out_specs=pl.BlockSpec((tm, D_MODEL), lambda i: (i, 0)),
        out_shape=jax.ShapeDtypeStruct((m, D_MODEL), F32),
        compiler_params=_params(("parallel",)),
        name="ffn",
    )(x2, g, w1, w2)


def _segment_ones():
    idx = np.arange(WIDTH) // HEAD_DIM
    return jnp.asarray(idx[:, None] == idx[None, :], BF16)


def _head_placement():
    src = np.arange(WIDTH)
    dst = (src // HEAD_DIM) * 128 + src % HEAD_DIM
    m = np.zeros((WIDTH, 2 * WIDTH), np.float32)
    m[src, dst] = 1.0
    return jnp.asarray(m, BF16)


def kernel(x, norm1_g, w_in, shift_mu, w0, w_up, a0, a_up, g_up, k_k, k_a, r_k, lnx_g, lnx_b,
           qn_g, kn_g, pool_w, pool_scale, w_branch, w_out, norm2_g, w_ff1, w_ff2):
    B, T, D = x.shape
    depth = w_in.shape[0]
    assert D == D_MODEL and T % MOBA_BLOCK == 0
    out_dtype = x.dtype
    x2 = x.astype(F32).reshape(B * T, D)
    seg = _segment_ones()
    place = _head_placement()
    row = lambda a: a.reshape(1, -1).astype(F32)
    zeros_lora = jnp.zeros((64, WIDTH), F32)

    for l in range(depth):
        w = w_in[l]
        w_mix = jnp.concatenate([w[:, :RWKV_IN], jnp.zeros((D, RWKV_PAD - RWKV_IN), w.dtype),
                                 w[:, RWKV_IN:RWKV_IN + 4 * WIDTH]], axis=1).astype(BF16)
        proj, gates = _inproj(x2, row(norm1_g[l]), w_mix, w[:, RWKV_IN + 4 * WIDTH:].astype(BF16))

        wup_pad = jnp.concatenate([w_up[l], zeros_lora], axis=0)
        aup_pad = jnp.concatenate([zeros_lora, a_up[l]], axis=0)
        y_rwkv = _rwkv(proj, B, T, row(shift_mu[l]), row(w0[l]), wup_pad, row(a0[l]), aup_pad, g_up[l],
                       row(k_k[l]), row(k_a[l]), row(r_k[l]), row(lnx_g[l]), row(lnx_b[l]), seg)

        qt, kaug, vt = _attn_prep(proj, B, T, row(jnp.tile(qn_g[l], ATTN_HEADS)),
                                  row(jnp.tile(kn_g[l], ATTN_HEADS)), seg, place)
        y_attn_t = _attention(qt, kaug, vt, B, T)

        x2 = _merge(x2, y_rwkv, y_attn_t, proj, gates, B, T, pool_w[l].astype(BF16), row(pool_scale[l]),
                    w_branch[l].astype(BF16), w_out[l].astype(BF16))
        x2 = _ffn(x2, row(norm2_g[l]), w_ff1[l].astype(BF16), w_ff2[l].astype(BF16))

    return x2.reshape(B, T, D).astype(out_dtype)
```

```python
import numpy as np
import jax
import jax.numpy as jnp
from jax import lax
from jax.experimental import pallas as pl
from jax.experimental.pallas import tpu as pltpu

F32 = jnp.float32
BF16 = jnp.bfloat16

D_MODEL = 1024
RWKV_HEADS = 8
HEAD_DIM = 64
WIDTH = RWKV_HEADS * HEAD_DIM
DECAY_LORA = 64
ICLR_LORA = 64
GATE_LORA = 128
ATTN_HEADS = 8
MOBA_BLOCK = 256
MOBA_TOP_K = 3
POOL_WINDOWS = (2, 4, 8, 16)
POOL_GROUP_DIM = 128
N_BRANCH = 3
D_FF = 4 * D_MODEL
RMS_EPS = 1e-6
GN_EPS = 64e-5
NEG_INF = -1e30

RWKV_IN = 3 * WIDTH + DECAY_LORA + ICLR_LORA + GATE_LORA
RWKV_PAD = 2048
ATTN_COL = RWKV_PAD // WIDTH
POOL_COL = ATTN_COL + 3
N_MIX = RWKV_PAD + 4 * WIDTH
N_GATE = N_BRANCH * D_MODEL

CHUNK = 64
RWKV_SEQS = 4
PACK = 4
VMEM_LIMIT = 56 * 1024 * 1024

NN = (((1,), (0,)), ((), ()))
NT = (((1,), (1,)), ((), ()))
TN = (((0,), (0,)), ((), ()))


def _dot(a, b, dims=NN):
    return lax.dot_general(a, b, dims, preferred_element_type=F32)


def _dot1(a, b, dims=NN):
    return _dot(a.astype(BF16), b.astype(BF16), dims)


def _split(a):
    hi = a.astype(BF16)
    lo = (a - hi.astype(F32)).astype(BF16)
    return hi, lo


def _dot3(a, b, dims=NN):
    ah, al = _split(a)
    bh, bl = _split(b)
    return _dot(ah, bh, dims) + (_dot(ah, bl, dims) + _dot(al, bh, dims))


def _params(sem):
    return pltpu.CompilerParams(dimension_semantics=sem, vmem_limit_bytes=VMEM_LIMIT)


def _inproj_kernel(x_ref, g_ref, wm_ref, wg_ref, om_ref, og_ref):
    x = x_ref[...]
    ms = jnp.mean(x * x, axis=-1, keepdims=True)
    h = (x * lax.rsqrt(ms + RMS_EPS) * g_ref[...]).astype(BF16)
    tn = D_MODEL
    for n in range(N_MIX // tn):
        om_ref[:, n * tn:(n + 1) * tn] = _dot(h, wm_ref[:, n * tn:(n + 1) * tn])
    for n in range(N_GATE // tn):
        og_ref[:, n * tn:(n + 1) * tn] = _dot(h, wg_ref[:, n * tn:(n + 1) * tn]).astype(BF16)


def _inproj(x2, g, w_mix, w_gate, tm=512):
    m = x2.shape[0]
    tm = min(tm, m)
    resident = lambda n: pl.BlockSpec((D_MODEL, n), lambda i: (0, 0), pipeline_mode=pl.Buffered(1))
    return pl.pallas_call(
        _inproj_kernel,
        grid=(m // tm,),
        in_specs=[
            pl.BlockSpec((tm, D_MODEL), lambda i: (i, 0)),
            pl.BlockSpec((1, D_MODEL), lambda i: (0, 0)),
            resident(N_MIX),
            resident(N_GATE),
        ],
        out_specs=[pl.BlockSpec((tm, N_MIX), lambda i: (i, 0)), pl.BlockSpec((tm, N_GATE), lambda i: (i, 0))],
        out_shape=[jax.ShapeDtypeStruct((m, N_MIX), F32), jax.ShapeDtypeStruct((m, N_GATE), BF16)],
        compiler_params=_params(("parallel",)),
        name="inproj",
    )(x2, g, w_mix, w_gate)


RWKV_STAGED = ("at", "rt", "kt", "bt", "kh", "bh", "v")


def _rwkv_kernel(p_ref, prev_ref, mu_ref, w0_ref, wup_ref, a0_ref, aup_ref, gup_ref,
                 kk_ref, ka_ref, rk_ref, lng_ref, lnb_ref, seg_ref, y_ref,
                 s_scr, at_scr, rt_scr, kt_scr, bt_scr, kh_scr, bh_scr, v_scr, g_scr, bv_scr, pl_scr):
    step = pl.program_id(1)
    nc = pl.num_programs(1) - 1
    L = CHUNK
    W = WIDTH
    nseq = p_ref.shape[0]
    R = nseq * L
    staged = dict(zip(RWKV_STAGED, (at_scr, rt_scr, kt_scr, bt_scr, kh_scr, bh_scr, v_scr)))

    @pl.when(step == 0)
    def _():
        for ref in (s_scr, g_scr, bv_scr, pl_scr) + tuple(staged.values()):
            ref[...] = jnp.zeros_like(ref)

    PW = PACK * HEAD_DIM
    rowp = lax.broadcasted_iota(jnp.int32, (L, PW), 0)
    lanep = lax.broadcasted_iota(jnp.int32, (L, PW), 1)
    colp = lanep & (HEAD_DIM - 1)
    head_of_lane = lanep >> 6
    strict = rowp > colp
    incl = rowp >= colp
    eye = (rowp == colp).astype(F32)
    blk16 = (rowp >> 4) == (colp >> 4)
    blk32 = (rowp >> 5) == (colp >> 5)

    half_of_lane = lax.broadcasted_iota(jnp.int32, (L, 128), 1) >> 6
    zero_tile = jnp.zeros((L, 128), BF16)

    def expand(x):
        xb = x.astype(BF16)
        copies = []
        for a in range(PACK):
            kept = jnp.where(half_of_lane == (a % 2), xb[:, (a // 2) * 128:(a // 2 + 1) * 128], zero_tile)
            copies.append(jnp.concatenate([kept, zero_tile] if a < 2 else [zero_tile, kept], axis=1))
        return jnp.concatenate(copies, axis=0)

    def mm(a, x_exp):
        return _dot(a.astype(BF16), x_exp)

    def mm_nt(a, b_exp):
        return _dot(a.astype(BF16), b_exp, NT)

    def mm_tn(a, b):
        full = _dot(a.astype(BF16), b.astype(BF16), TN)
        out = jnp.where(head_of_lane == 0, full[0:HEAD_DIM], 0.0)
        for hh in range(1, PACK):
            out = out + jnp.where(head_of_lane == hh, full[hh * HEAD_DIM:(hh + 1) * HEAD_DIM], 0.0)
        return out

    first_chunk = jnp.minimum(step, nc - 1) == 0

    def mixed(c0, c1):
        out = []
        for b in range(nseq):
            pc = p_ref[b, :, c0:c1]
            first = jnp.where(first_chunk, 0.0, prev_ref[b, 7:8, c0:c1])
            shifted = jnp.where(rows_like(pc) == 0, first, pltpu.roll(pc, 1, axis=0))
            out.append(pc + (shifted - pc) * mu_ref[:, c0:c1])
        return jnp.concatenate(out, axis=0)

    def rows_like(x):
        return lax.broadcasted_iota(jnp.int32, x.shape, 0)

    def seg_sum(z):
        seg_pw = seg_ref[0:PW, 0:PW]
        return jnp.concatenate([_dot1(z[:, c:c + PW], seg_pw) for c in range(0, W, PW)], axis=1)

    pre = {}

    def prep_decay():
        lo = mixed(3 * W, 3 * W + 128)
        pre["lo"] = lo
        wpre = w0_ref[...] + _dot3(jnp.tanh(lo), wup_ref[...])
        pre["logw"] = -float(np.exp(-0.5)) * jax.nn.sigmoid(wpre)

    def prep_gates():
        xg = mixed(3 * W + 128, 3 * W + 256)
        pre["a"] = jax.nn.sigmoid(a0_ref[...] + _dot1(pre["lo"], aup_ref[...]))
        g_scr[...] = _dot1(jax.nn.sigmoid(xg), gup_ref[...])

    def prep_cumsum():
        logw = pre["logw"]
        tl = rows_like(logw) & (L - 1)
        cl = logw
        sh = 1
        while sh < L:
            cl = cl + jnp.where(tl >= sh, pltpu.roll(cl, sh, axis=0), 0.0)
            sh *= 2
        pre["cl"] = cl

    def prep_decay_factors():
        cl = pre["cl"]
        cl_last = [cl[(b + 1) * L - 1:(b + 1) * L, :] for b in range(nseq)]
        cl_end = cl_last[nseq - 1]
        rows = rows_like(cl)
        for b in range(nseq - 2, -1, -1):
            cl_end = jnp.where(rows < (b + 1) * L, cl_last[b], cl_end)
        pre["e_neg"] = jnp.exp(-cl)
        pre["e_rem"] = jnp.exp(cl_end - cl)
        for b in range(nseq):
            pl_scr[b] = jnp.exp(cl_last[b])

    def prep_k():
        k = mixed(W, 2 * W)
        kk = k * kk_ref[...]
        ss = seg_sum(kk * kk)
        pre["kk"] = kk * lax.rsqrt(jnp.maximum(ss, 1e-24))
        pre["k2"] = k * (1.0 + (pre["a"] - 1.0) * ka_ref[...])

    def prep_ab():
        bvec = pre["kk"] * pre["a"]
        at_scr[...] = (-pre["kk"] * jnp.exp(pre["cl"] - pre["logw"])).astype(BF16)
        bt_scr[...] = (bvec * pre["e_neg"]).astype(BF16)
        bh_scr[...] = (bvec * pre["e_rem"]).astype(BF16)

    def prep_kk():
        kt_scr[...] = (pre["k2"] * pre["e_neg"]).astype(BF16)
        kh_scr[...] = (pre["k2"] * pre["e_rem"]).astype(BF16)

    def prep_r():
        r = mixed(0, W)
        rt_scr[...] = (r * jnp.exp(pre["cl"])).astype(BF16)
        pre["rk"] = r * pre["k2"] * rk_ref[...]

    def prep_v():
        v = mixed(2 * W, 3 * W)
        v_scr[...] = v.astype(BF16)
        bv_scr[...] = seg_sum(pre["rk"]) * v

    pieces = [prep_decay, prep_gates, prep_cumsum, prep_decay_factors, prep_k, prep_ab, prep_kk, prep_r, prep_v]

    def interleave():
        if pieces:
            pieces.pop(0)()

    items = [(b, gq) for b in range(nseq) for gq in range(RWKV_HEADS // PACK)]
    n = range(len(items))
    rs = [slice(b * L, (b + 1) * L) for b, _ in items]
    cs = [slice(gq * PW, (gq + 1) * PW) for _, gq in items]
    each = lambda f: [f(i) for i in n]
    ld = lambda name: each(lambda i: staged[name][rs[i], cs[i]])

    s0 = each(lambda i: jnp.where(step <= 1, 0.0, s_scr[i]))
    vh = ld("v")
    at, rt, kt, bt, kh, bh = ld("at"), ld("rt"), ld("kt"), ld("bt"), ld("kh"), ld("bh")
    gate = each(lambda i: g_scr[rs[i], cs[i]])
    bonus_v = each(lambda i: bv_scr[rs[i], cs[i]])
    p_last = each(lambda i: pl_scr[items[i][0], :, cs[i]])

    v_exp = each(lambda i: expand(vh[i]))
    ar = each(lambda i: jnp.concatenate([at[i], rt[i]], axis=0))
    mk = each(lambda i: mm_nt(ar[i], expand(kt[i])))
    mb = each(lambda i: mm_nt(ar[i], expand(bt[i])))
    m_ab = each(lambda i: jnp.where(strict, mb[i][:L], 0.0))
    m_kk = each(lambda i: jnp.concatenate([jnp.where(strict, mk[i][:L], 0.0),
                                           jnp.where(incl, mk[i][L:], 0.0)], axis=0))
    m_rb = each(lambda i: jnp.where(incl, mb[i][L:], 0.0))
    interleave()

    d1 = each(lambda i: jnp.where(blk16, m_ab[i], 0.0))
    e1 = each(lambda i: jnp.where(blk32 & jnp.logical_not(blk16), m_ab[i], 0.0))
    e2 = each(lambda i: jnp.where(jnp.logical_not(blk32), m_ab[i], 0.0))
    x = each(lambda i: eye + d1[i])
    d2 = each(lambda i: mm(d1[i], expand(d1[i])))
    ars = each(lambda i: mm_nt(ar[i], expand(s0[i])))
    kv = each(lambda i: mm(m_kk[i], v_exp[i]))
    interleave()
    x = each(lambda i: x[i] + mm(d2[i], expand(x[i])))
    d4 = each(lambda i: mm(d2[i], expand(d2[i])))
    interleave()
    x = each(lambda i: x[i] + mm(d4[i], expand(x[i])))
    d8 = each(lambda i: mm(d4[i], expand(d4[i])))
    interleave()
    x = each(lambda i: x[i] + mm(d8[i], expand(x[i])))
    t = each(lambda i: mm(e1[i], expand(x[i])))
    interleave()
    x = each(lambda i: x[i] + mm(x[i], expand(t[i])))
    t = each(lambda i: mm(e2[i], expand(x[i])))
    interleave()
    x = each(lambda i: x[i] + mm(x[i], expand(t[i])))
    interleave()
    u = each(lambda i: mm(x[i], expand(ars[i][:L] + kv[i][:L])))
    interleave()
    o = each(lambda i: ars[i][L:] + kv[i][L:] + mm(m_rb[i], expand(u[i])))
    s_add = each(lambda i: mm_tn(jnp.concatenate([vh[i], u[i].astype(BF16)], axis=0),
                                 jnp.concatenate([kh[i], bh[i]], axis=0)))
    interleave()
    for i in n:
        s_scr[i] = s0[i] * p_last[i] + s_add[i]

    seg = seg_ref[0:PW, 0:PW]
    inv_n = 1.0 / HEAD_DIM
    for i in n:
        mean = _dot1(o[i], seg) * inv_n
        cen = o[i] - mean
        var = _dot1(cen * cen, seg) * inv_n
        on = cen * lax.rsqrt(var + GN_EPS) * lng_ref[:, cs[i]] + lnb_ref[:, cs[i]]
        y_ref[items[i][0], :, cs[i]] = (on + bonus_v[i]) * gate[i]
    while pieces:
        interleave()


def _rwkv(proj, B, T, mu, w0, wup_pad, a0, aup_pad, gup, k_k, k_a, r_k, lng, lnb, seg):
    L = CHUNK
    nc = T // L
    nseq = RWKV_SEQS if B % RWKV_SEQS == 0 else 1
    R = nseq * L
    proj3 = proj.reshape(B, T, N_MIX)
    vec = lambda n: pl.BlockSpec((1, n), lambda b, s: (0, 0))
    mat = lambda m, n: pl.BlockSpec((m, n), lambda b, s: (0, 0))
    cur = lambda s: jnp.minimum(s, nc - 1)
    y = pl.pallas_call(
        _rwkv_kernel,
        grid=(B // nseq, nc + 1),
        in_specs=[
            pl.BlockSpec((nseq, L, RWKV_IN), lambda b, s: (b, cur(s), 0)),
            pl.BlockSpec((nseq, 8, RWKV_IN), lambda b, s: (b, jnp.maximum(cur(s) * (L // 8) - 1, 0), 0)),
            vec(RWKV_IN), vec(WIDTH), mat(128, WIDTH), vec(WIDTH), mat(128, WIDTH), mat(GATE_LORA, WIDTH),
            vec(WIDTH), vec(WIDTH), vec(WIDTH), vec(WIDTH), vec(WIDTH), mat(WIDTH, WIDTH),
        ],
        out_specs=pl.BlockSpec((nseq, L, WIDTH), lambda b, s: (b, jnp.maximum(s - 1, 0), 0)),
        out_shape=jax.ShapeDtypeStruct((B, T, WIDTH), F32),
        scratch_shapes=[pltpu.VMEM((nseq * RWKV_HEADS // PACK, HEAD_DIM, PACK * HEAD_DIM), F32)]
        + [pltpu.VMEM((R, WIDTH), BF16)] * len(RWKV_STAGED)
        + [pltpu.VMEM((R, WIDTH), F32), pltpu.VMEM((R, WIDTH), F32), pltpu.VMEM((nseq, 1, WIDTH), F32)],
        compiler_params=_params(("parallel", "arbitrary")),
        name="rwkv",
    )(proj3, proj3, mu, w0, wup_pad, a0, aup_pad, gup, k_k, k_a, r_k, lng, lnb, seg)
    return y.reshape(B * T, WIDTH)


def _bf16_split3(c):
    out, r = [], float(c)
    for _ in range(3):
        h = float(np.asarray(r, np.float32).astype(BF16).astype(np.float32))
        out.append(h)
        r -= h
    return out


LOG2E = float(np.log2(np.e))
Q_ROWS = 128
BIAS_ROW = HEAD_DIM
PEN_ROW = HEAD_DIM + 8
V_ROWS = HEAD_DIM + 16


def _attn_kernel(q_ref, k_ref, v_ref, qg_ref, kg_ref, seg_ref, place_ref, o_ref,
                 kmean_scr, k_scr, v_scr, sa_scr, sb_scr, ma_scr, mb_scr, m_scr, acc_scr):
    i = pl.program_id(1)
    nbp = kmean_scr.shape[0]
    BLK = MOBA_BLOCK

    @pl.when(i == 0)
    def _():
        kmean_scr[...] = jnp.zeros_like(kmean_scr)

    inv_n = 1.0 / HEAD_DIM
    q = q_ref[...]
    k = k_ref[...]
    half = WIDTH // 2
    seg_half = seg_ref[0:half, 0:half]
    seg_sum = lambda z: jnp.concatenate([_dot1(z[:, :half], seg_half), _dot1(z[:, half:], seg_half)], axis=1)
    qn = q * lax.rsqrt(seg_sum(q * q) * inv_n + RMS_EPS) * qg_ref[...]
    kn = k * lax.rsqrt(seg_sum(k * k) * inv_n + RMS_EPS) * kg_ref[...]
    qt = (qn * (LOG2E * HEAD_DIM ** -0.5)).T

    kmean = jnp.mean(kn, axis=0, keepdims=True)
    rown = lax.broadcasted_iota(jnp.int32, kmean_scr.shape, 0)
    kmeans = jnp.where(rown == i, kmean, kmean_scr[...])
    kmean_scr[...] = kmeans

    kaug = _dot(kn.astype(BF16), place_ref[...])
    lane = lax.broadcasted_iota(jnp.int32, kaug.shape, 1) & 127
    pos = lax.broadcasted_iota(jnp.int32, kaug.shape, 0).astype(F32)
    kaug = jnp.where((lane >= BIAS_ROW) & (lane < BIAS_ROW + 3), pos, kaug)
    kaug = jnp.where((lane >= BIAS_ROW + 3) & (lane < BIAS_ROW + 6), i.astype(F32), kaug)
    kaug = jnp.where(lane == PEN_ROW + i, 1.0, kaug).astype(BF16)

    vt = v_ref[...].T
    sub16 = lax.broadcasted_iota(jnp.int32, (V_ROWS - HEAD_DIM, BLK), 0)
    ones_rows = jnp.where(sub16 == 0, 1.0, 0.0)

    blk_id = lax.broadcasted_iota(jnp.int32, (nbp, BLK), 0)
    sub8 = lax.broadcasted_iota(jnp.int32, (8, BLK), 0)
    tail = jnp.zeros((Q_ROWS - PEN_ROW - nbp, BLK), F32)
    q = []
    for h in range(ATTN_HEADS):
        sl = slice(h * HEAD_DIM, (h + 1) * HEAD_DIM)
        qth = qt[sl, :]

        gate = _dot3(kmeans[:, sl], qth)
        gate = jnp.where(blk_id < i, gate, NEG_INF)
        chosen = blk_id == i
        for _ in range(MOBA_TOP_K):
            mx = jnp.max(gate, axis=0, keepdims=True)
            idx = jnp.min(jnp.where(gate == mx, blk_id, nbp), axis=0, keepdims=True)
            hit = blk_id == idx
            chosen = chosen | (hit & (blk_id < i))
            gate = jnp.where(hit, -jnp.inf, gate)
        penalty = jnp.where(chosen, 0.0, NEG_INF)

        slope = 2.0 ** (-8.0 * (h + 1) / ATTN_HEADS)
        cvals = _bf16_split3(slope * LOG2E) + _bf16_split3(slope * LOG2E * MOBA_BLOCK)
        consts = jnp.zeros((8, BLK), F32)
        for r, cv in enumerate(cvals):
            consts = jnp.where(sub8 == r, cv, consts)
        q.append(jnp.concatenate([qth, consts, penalty, tail], axis=0).astype(BF16))
        k_scr[h, i] = kaug[:, h * 128:(h + 1) * 128]
        v_scr[h, i] = jnp.concatenate([vt[sl, :], ones_rows], axis=0).astype(BF16)

    rowk = lax.broadcasted_iota(jnp.int32, (BLK, BLK), 0)
    colq = lax.broadcasted_iota(jnp.int32, (BLK, BLK), 1)
    causal = rowk <= colq

    heads = range(ATTN_HEADS)

    def scores_into(dst, h, j):
        s_h = _dot(k_scr[h,j], q[h])
        dst[0][h] = s_h
        dst[1][h] = jnp.max(s_h, axis=0, keepdims=True)

    def softmax_update(h, src, j):
        m_old = m_scr[h]
        m_new = jnp.maximum(m_old, src[1][h])
        alpha = jnp.exp2(m_old - m_new)
        p = jnp.exp2(src[0][h] - m_new).astype(BF16)
        m_scr[h] = m_new
        acc_scr[h] = acc_scr[h] * alpha + _dot(v_scr[h,j], p)

    def fused_step(dst, j_next, src, j):
        for h in heads:
            scores_into(dst, h, j_next)
            softmax_update(h, src, j)

    buf_a = (sa_scr, ma_scr)
    buf_b = (sb_scr, mb_scr)

    own = lambda h: _dot(k_scr[h,i], q[h])
    s_next = own(0)
    for h in heads:
        s_h = jnp.where(causal, s_next, NEG_INF)
        if h + 1 < ATTN_HEADS:
            s_next = own(h + 1)
        scores_into(buf_a, h, 0)
        m0 = jnp.max(s_h, axis=0, keepdims=True)
        m_scr[h] = m0
        acc_scr[h] = _dot(v_scr[h,i], jnp.exp2(s_h - m0).astype(BF16))

    def two_blocks(jj, carry):
        j = 2 * jj
        fused_step(buf_b, j + 1, buf_a, j)
        fused_step(buf_a, jnp.minimum(j + 2, i - 1), buf_b, j + 1)
        return carry

    lax.fori_loop(0, i // 2, two_blocks, 0)

    @pl.when(i % 2 == 1)
    def _():
        for h in heads:
            softmax_update(h, buf_a, i - 1)

    for h in heads:
        acc = acc_scr[h]
        o_ref[0, h * HEAD_DIM:(h + 1) * HEAD_DIM, :] = acc[:HEAD_DIM] / acc[HEAD_DIM:HEAD_DIM + 1]


def _attention(proj, B, T, qg, kg, seg, place):
    nb = T // MOBA_BLOCK
    nbp = -(-nb // 8) * 8
    assert PEN_ROW + nbp <= Q_ROWS
    H = ATTN_HEADS
    col = lambda c: pl.BlockSpec((MOBA_BLOCK, WIDTH), lambda b, i: (b * nb + i, c))
    const = lambda m, n: pl.BlockSpec((m, n), lambda b, i: (0, 0))
    tile = lambda r: pltpu.VMEM((H, r, MOBA_BLOCK), F32)
    return pl.pallas_call(
        _attn_kernel,
        grid=(B, nb),
        in_specs=[col(ATTN_COL), col(ATTN_COL + 1), col(ATTN_COL + 2),
                  const(1, WIDTH), const(1, WIDTH), const(WIDTH, WIDTH), const(WIDTH, 2 * WIDTH)],
        out_specs=pl.BlockSpec((1, WIDTH, MOBA_BLOCK), lambda b, i: (b, 0, i)),
        out_shape=jax.ShapeDtypeStruct((B, WIDTH, T), F32),
        scratch_shapes=[pltpu.VMEM((nbp, WIDTH), F32),
                        pltpu.VMEM((H, nb, MOBA_BLOCK, 128), BF16), pltpu.VMEM((H, nb, V_ROWS, MOBA_BLOCK), BF16),
                        tile(MOBA_BLOCK), tile(MOBA_BLOCK), tile(1), tile(1), tile(1), tile(V_ROWS)],
        compiler_params=_params(("parallel", "arbitrary")),
        name="moba_attention",
    )(proj, proj, proj, qg, kg, seg, place)


POOL_HALO = 16


def _merge_kernel(x_ref, yr_ref, yat_ref, pp_ref, pprev_ref, g0_ref, g1_ref, g2_ref,
                  pw_ref, pscale_ref, wb_ref, wo_ref, n2_ref, w1_ref, w2_ref, o_ref):
    t = pl.program_id(1)
    tl = x_ref.shape[0]

    u = pp_ref[...]
    halo = jnp.where(t == 0, 0.0, pprev_ref[...])
    ext = jnp.concatenate([halo, u], axis=0)
    pos = (t * tl + lax.broadcasted_iota(jnp.int32, (tl, POOL_GROUP_DIM), 0) + 1).astype(F32)
    y_pool = []
    for gi, w in enumerate(POOL_WINDOWS):
        sl = slice(gi * POOL_GROUP_DIM, (gi + 1) * POOL_GROUP_DIM)
        acc = ext[:, sl]
        s = 1
        while s < w:
            acc = acc + pltpu.roll(acc, s, axis=0)
            s *= 2
        d = acc[POOL_HALO:] / jnp.minimum(pos, float(w)) - u[:, sl]
        y_pool.append(_dot(d.astype(BF16), pw_ref[gi]))
    y_pool = jnp.concatenate(y_pool, axis=1) * pscale_ref[...]

    br_r = _dot(yr_ref[...].astype(BF16), wb_ref[0])
    br_a = _dot(yat_ref[0].astype(BF16), wb_ref[1], TN)
    br_p = _dot(y_pool.astype(BF16), wb_ref[2])
    gate = lambda ref: jax.nn.sigmoid(ref[...].astype(F32))
    merged = gate(g0_ref) * br_r + gate(g1_ref) * br_a + gate(g2_ref) * br_p
    x = x_ref[...] + _dot(merged.astype(BF16), wo_ref[...])

    ms = jnp.mean(x * x, axis=-1, keepdims=True)
    h = (x * lax.rsqrt(ms + RMS_EPS) * n2_ref[...]).astype(BF16)
    tf = D_MODEL
    for f in range(D_FF // tf):
        a = jnp.maximum(_dot(h, w1_ref[:, f * tf:(f + 1) * tf]), 0.0)
        x = x + _dot((a * a).astype(BF16), w2_ref[f * tf:(f + 1) * tf, :])
    o_ref[...] = x


def _merge(x2, y_rwkv, y_attn_t, proj, gates, B, T, pool_w, pool_scale, w_branch, w_out, norm2, w1, w2, tl=512):
    tl = min(tl, T)
    nt = T // tl
    rows = lambda w, c: pl.BlockSpec((tl, w), lambda b, t: (b * nt + t, c))
    const2 = lambda m, n: pl.BlockSpec((m, n), lambda b, t: (0, 0), pipeline_mode=pl.Buffered(1))
    const3 = lambda a, m, n: pl.BlockSpec((a, m, n), lambda b, t: (0, 0, 0), pipeline_mode=pl.Buffered(1))
    return pl.pallas_call(
        _merge_kernel,
        grid=(B, nt),
        in_specs=[
            rows(D_MODEL, 0),
            rows(WIDTH, 0),
            pl.BlockSpec((1, WIDTH, tl), lambda b, t: (b, 0, t)),
            rows(WIDTH, POOL_COL),
            pl.BlockSpec((POOL_HALO, WIDTH),
                         lambda b, t: (jnp.maximum((b * nt + t) * (tl // POOL_HALO) - 1, 0), POOL_COL)),
            rows(D_MODEL, 0), rows(D_MODEL, 1), rows(D_MODEL, 2),
            const3(len(POOL_WINDOWS), POOL_GROUP_DIM, POOL_GROUP_DIM),
            const2(1, WIDTH),
            const3(N_BRANCH, WIDTH, D_MODEL),
            const2(D_MODEL, D_MODEL),
            const2(1, D_MODEL),
            const2(D_MODEL, D_FF),
            const2(D_FF, D_MODEL),
        ],
        out_specs=rows(D_MODEL, 0),
        out_shape=jax.ShapeDtypeStruct((B * T, D_MODEL), F32),
        compiler_params=_params(("parallel", "arbitrary")),
        name="merge_mlp",
    )(x2, y_rwkv, y_attn_t, proj, proj, gates, gates, gates, pool_w, pool_scale, w_branch, w_out, norm2, w1, w2)


def _segment_ones():
    idx = np.arange(WIDTH) // HEAD_DIM
    return jnp.asarray(idx[:, None] == idx[None, :], BF16)


def _head_placement():
    src = np.arange(WIDTH)
    dst = (src // HEAD_DIM) * 128 + src % HEAD_DIM
    m = np.zeros((WIDTH, 2 * WIDTH), np.float32)
    m[src, dst] = 1.0
    return jnp.asarray(m, BF16)


def kernel(x, norm1_g, w_in, shift_mu, w0, w_up, a0, a_up, g_up, k_k, k_a, r_k, lnx_g, lnx_b,
           qn_g, kn_g, pool_w, pool_scale, w_branch, w_out, norm2_g, w_ff1, w_ff2):
    B, T, D = x.shape
    depth = w_in.shape[0]
    assert D == D_MODEL and T % MOBA_BLOCK == 0
    out_dtype = x.dtype
    x2 = x.astype(F32).reshape(B * T, D)
    seg = _segment_ones()
    place = _head_placement()
    row = lambda a: a.reshape(1, -1).astype(F32)
    zeros_lora = jnp.zeros((64, WIDTH), F32)

    for l in range(depth):
        w = w_in[l]
        w_mix = jnp.concatenate([w[:, :RWKV_IN], jnp.zeros((D, RWKV_PAD - RWKV_IN), w.dtype),
                                 w[:, RWKV_IN:RWKV_IN + 4 * WIDTH]], axis=1).astype(BF16)
        proj, gates = _inproj(x2, row(norm1_g[l]), w_mix, w[:, RWKV_IN + 4 * WIDTH:].astype(BF16))

        wup_pad = jnp.concatenate([w_up[l], zeros_lora], axis=0)
        aup_pad = jnp.concatenate([zeros_lora, a_up[l]], axis=0)
        y_rwkv = _rwkv(proj, B, T, row(shift_mu[l]), row(w0[l]), wup_pad, row(a0[l]), aup_pad, g_up[l],
                       row(k_k[l]), row(k_a[l]), row(r_k[l]), row(lnx_g[l]), row(lnx_b[l]), seg)

        y_attn_t = _attention(proj, B, T, row(jnp.tile(qn_g[l], ATTN_HEADS)),
                              row(jnp.tile(kn_g[l], ATTN_HEADS)), seg, place)

        x2 = _merge(x2, y_rwkv, y_attn_t, proj, gates, B, T, pool_w[l].astype(BF16), row(pool_scale[l]),
                    w_branch[l].astype(BF16), w_out[l].astype(BF16),
                    row(norm2_g[l]), w_ff1[l].astype(BF16), w_ff2[l].astype(BF16))

    return x2.reshape(B, T, D).astype(out_dtype)
```

```python
import numpy as np
import jax
import jax.numpy as jnp
from jax import lax
from jax.experimental import pallas as pl
from jax.experimental.pallas import tpu as pltpu

F32 = jnp.float32
BF16 = jnp.bfloat16

D_MODEL = 1024
RWKV_HEADS = 8
HEAD_DIM = 64
WIDTH = RWKV_HEADS * HEAD_DIM
DECAY_LORA = 64
ICLR_LORA = 64
GATE_LORA = 128
ATTN_HEADS = 8
MOBA_BLOCK = 256
MOBA_TOP_K = 3
POOL_WINDOWS = (2, 4, 8, 16)
POOL_GROUP_DIM = 128
N_BRANCH = 3
D_FF = 4 * D_MODEL
RMS_EPS = 1e-6
GN_EPS = 64e-5
NEG_INF = -1e30

RWKV_IN = 3 * WIDTH + DECAY_LORA + ICLR_LORA + GATE_LORA
RWKV_PAD = 2048
ATTN_COL = RWKV_PAD // WIDTH
POOL_COL = ATTN_COL + 3
N_MIX = RWKV_PAD + 4 * WIDTH
N_GATE = N_BRANCH * D_MODEL

CHUNK = 64
RWKV_SEQS = 4
PACK = 4
VMEM_LIMIT = 56 * 1024 * 1024

NN = (((1,), (0,)), ((), ()))
NT = (((1,), (1,)), ((), ()))
TN = (((0,), (0,)), ((), ()))


def _dot(a, b, dims=NN):
    return lax.dot_general(a, b, dims, preferred_element_type=F32)


def _dot1(a, b, dims=NN):
    return _dot(a.astype(BF16), b.astype(BF16), dims)


def _split(a):
    hi = a.astype(BF16)
    lo = (a - hi.astype(F32)).astype(BF16)
    return hi, lo


def _dot3(a, b, dims=NN):
    ah, al = _split(a)
    bh, bl = _split(b)
    return _dot(ah, bh, dims) + (_dot(ah, bl, dims) + _dot(al, bh, dims))


def _params(sem):
    return pltpu.CompilerParams(dimension_semantics=sem, vmem_limit_bytes=VMEM_LIMIT)


def _inproj_kernel(x_ref, g_ref, wm_ref, wg_ref, om_ref, og_ref):
    x = x_ref[...]
    ms = jnp.mean(x * x, axis=-1, keepdims=True)
    h = (x * lax.rsqrt(ms + RMS_EPS) * g_ref[...]).astype(BF16)
    tn = D_MODEL
    for n in range(N_MIX // tn):
        om_ref[:, n * tn:(n + 1) * tn] = _dot(h, wm_ref[:, n * tn:(n + 1) * tn])
    for n in range(N_GATE // tn):
        og_ref[:, n * tn:(n + 1) * tn] = _dot(h, wg_ref[:, n * tn:(n + 1) * tn]).astype(BF16)


def _inproj(x2, g, w_mix, w_gate, tm=512):
    m = x2.shape[0]
    tm = min(tm, m)
    resident = lambda n: pl.BlockSpec((D_MODEL, n), lambda i: (0, 0), pipeline_mode=pl.Buffered(1))
    return pl.pallas_call(
        _inproj_kernel,
        grid=(m // tm,),
        in_specs=[
            pl.BlockSpec((tm, D_MODEL), lambda i: (i, 0)),
            pl.BlockSpec((1, D_MODEL), lambda i: (0, 0)),
            resident(N_MIX),
            resident(N_GATE),
        ],
        out_specs=[pl.BlockSpec((tm, N_MIX), lambda i: (i, 0)), pl.BlockSpec((tm, N_GATE), lambda i: (i, 0))],
        out_shape=[jax.ShapeDtypeStruct((m, N_MIX), F32), jax.ShapeDtypeStruct((m, N_GATE), BF16)],
        compiler_params=_params(("parallel",)),
        name="inproj",
    )(x2, g, w_mix, w_gate)


RWKV_STAGED = ("at", "rt", "kt", "bt", "kh", "bh", "v")


def _rwkv_kernel(p_ref, prev_ref, mu_ref, w0_ref, wup_ref, a0_ref, aup_ref, gup_ref,
                 kk_ref, ka_ref, rk_ref, lng_ref, lnb_ref, seg_ref, y_ref,
                 s_scr, at_scr, rt_scr, kt_scr, bt_scr, kh_scr, bh_scr, v_scr, g_scr, bv_scr, pl_scr):
    step = pl.program_id(1)
    nc = pl.num_programs(1) - 1
    L = CHUNK
    W = WIDTH
    nseq = p_ref.shape[0]
    R = nseq * L
    staged = dict(zip(RWKV_STAGED, (at_scr, rt_scr, kt_scr, bt_scr, kh_scr, bh_scr, v_scr)))

    @pl.when(step == 0)
    def _():
        for ref in (s_scr, g_scr, bv_scr, pl_scr) + tuple(staged.values()):
            ref[...] = jnp.zeros_like(ref)

    PW = PACK * HEAD_DIM
    rowp = lax.broadcasted_iota(jnp.int32, (L, PW), 0)
    lanep = lax.broadcasted_iota(jnp.int32, (L, PW), 1)
    colp = lanep & (HEAD_DIM - 1)
    head_of_lane = lanep >> 6
    strict = rowp > colp
    incl = rowp >= colp
    eye = (rowp == colp).astype(F32)
    blk16 = (rowp >> 4) == (colp >> 4)
    blk32 = (rowp >> 5) == (colp >> 5)

    half_of_lane = lax.broadcasted_iota(jnp.int32, (L, 128), 1) >> 6
    zero_tile = jnp.zeros((L, 128), BF16)

    def expand(x):
        xb = x.astype(BF16)
        copies = []
        for a in range(PACK):
            kept = jnp.where(half_of_lane == (a % 2), xb[:, (a // 2) * 128:(a // 2 + 1) * 128], zero_tile)
            copies.append(jnp.concatenate([kept, zero_tile] if a < 2 else [zero_tile, kept], axis=1))
        return jnp.concatenate(copies, axis=0)

    def mm(a, x_exp):
        return _dot(a.astype(BF16), x_exp)

    def mm_nt(a, b_exp):
        return _dot(a.astype(BF16), b_exp, NT)

    def mm_tn(a, b):
        full = _dot(a.astype(BF16), b.astype(BF16), TN)
        out = jnp.where(head_of_lane == 0, full[0:HEAD_DIM], 0.0)
        for hh in range(1, PACK):
            out = out + jnp.where(head_of_lane == hh, full[hh * HEAD_DIM:(hh + 1) * HEAD_DIM], 0.0)
        return out

    first_chunk = jnp.minimum(step, nc - 1) == 0

    def mixed(c0, c1):
        out = []
        for b in range(nseq):
            pc = p_ref[b, :, c0:c1]
            first = jnp.where(first_chunk, 0.0, prev_ref[b, 7:8, c0:c1])
            shifted = jnp.where(rows_like(pc) == 0, first, pltpu.roll(pc, 1, axis=0))
            out.append(pc + (shifted - pc) * mu_ref[:, c0:c1])
        return jnp.concatenate(out, axis=0)

    def rows_like(x):
        return lax.broadcasted_iota(jnp.int32, x.shape, 0)

    def seg_sum(z):
        seg_pw = seg_ref[0:PW, 0:PW]
        return jnp.concatenate([_dot1(z[:, c:c + PW], seg_pw) for c in range(0, W, PW)], axis=1)

    pre = {}

    def prep_decay():
        lo = mixed(3 * W, 3 * W + 128)
        pre["lo"] = lo
        wpre = w0_ref[...] + _dot3(jnp.tanh(lo), wup_ref[...])
        pre["logw"] = -float(np.exp(-0.5)) * jax.nn.sigmoid(wpre)

    def prep_gates():
        xg = mixed(3 * W + 128, 3 * W + 256)
        pre["a"] = jax.nn.sigmoid(a0_ref[...] + _dot1(pre["lo"], aup_ref[...]))
        g_scr[...] = _dot1(jax.nn.sigmoid(xg), gup_ref[...])

    def prep_cumsum():
        logw = pre["logw"]
        tl = rows_like(logw) & (L - 1)
        cl = logw
        sh = 1
        while sh < L:
            cl = cl + jnp.where(tl >= sh, pltpu.roll(cl, sh, axis=0), 0.0)
            sh *= 2
        pre["cl"] = cl

    def prep_decay_factors():
        cl = pre["cl"]
        cl_last = [cl[(b + 1) * L - 1:(b + 1) * L, :] for b in range(nseq)]
        cl_end = cl_last[nseq - 1]
        rows = rows_like(cl)
        for b in range(nseq - 2, -1, -1):
            cl_end = jnp.where(rows < (b + 1) * L, cl_last[b], cl_end)
        pre["e_neg"] = jnp.exp(-cl)
        pre["e_rem"] = jnp.exp(cl_end - cl)
        for b in range(nseq):
            pl_scr[b] = jnp.exp(cl_last[b])

    def prep_k():
        k = mixed(W, 2 * W)
        kk = k * kk_ref[...]
        ss = seg_sum(kk * kk)
        pre["kk"] = kk * lax.rsqrt(jnp.maximum(ss, 1e-24))
        pre["k2"] = k * (1.0 + (pre["a"] - 1.0) * ka_ref[...])

    def prep_ab():
        bvec = pre["kk"] * pre["a"]
        at_scr[...] = (-pre["kk"] * jnp.exp(pre["cl"] - pre["logw"])).astype(BF16)
        bt_scr[...] = (bvec * pre["e_neg"]).astype(BF16)
        bh_scr[...] = (bvec * pre["e_rem"]).astype(BF16)

    def prep_kk():
        kt_scr[...] = (pre["k2"] * pre["e_neg"]).astype(BF16)
        kh_scr[...] = (pre["k2"] * pre["e_rem"]).astype(BF16)

    def prep_r():
        r = mixed(0, W)
        rt_scr[...] = (r * jnp.exp(pre["cl"])).astype(BF16)
        pre["rk"] = r * pre["k2"] * rk_ref[...]

    def prep_v():
        v = mixed(2 * W, 3 * W)
        v_scr[...] = v.astype(BF16)
        bv_scr[...] = seg_sum(pre["rk"]) * v

    pieces = [prep_decay, prep_gates, prep_cumsum, prep_decay_factors, prep_k, prep_ab, prep_kk, prep_r, prep_v]

    def interleave():
        if pieces:
            pieces.pop(0)()

    items = [(b, gq) for b in range(nseq) for gq in range(RWKV_HEADS // PACK)]
    n = range(len(items))
    rs = [slice(b * L, (b + 1) * L) for b, _ in items]
    cs = [slice(gq * PW, (gq + 1) * PW) for _, gq in items]
    each = lambda f: [f(i) for i in n]
    ld = lambda name: each(lambda i: staged[name][rs[i], cs[i]])

    s0 = each(lambda i: jnp.where(step <= 1, 0.0, s_scr[i]))
    vh = ld("v")
    at, rt, kt, bt, kh, bh = ld("at"), ld("rt"), ld("kt"), ld("bt"), ld("kh"), ld("bh")
    gate = each(lambda i: g_scr[rs[i], cs[i]])
    bonus_v = each(lambda i: bv_scr[rs[i], cs[i]])
    p_last = each(lambda i: pl_scr[items[i][0], :, cs[i]])

    v_exp = each(lambda i: expand(vh[i]))
    ar = each(lambda i: jnp.concatenate([at[i], rt[i]], axis=0))
    mk = each(lambda i: mm_nt(ar[i], expand(kt[i])))
    mb = each(lambda i: mm_nt(ar[i], expand(bt[i])))
    m_ab = each(lambda i: jnp.where(strict, mb[i][:L], 0.0))
    m_kk = each(lambda i: jnp.concatenate([jnp.where(strict, mk[i][:L], 0.0),
                                           jnp.where(incl, mk[i][L:], 0.0)], axis=0))
    m_rb = each(lambda i: jnp.where(incl, mb[i][L:], 0.0))
    interleave()

    d1 = each(lambda i: jnp.where(blk16, m_ab[i], 0.0))
    e1 = each(lambda i: jnp.where(blk32 & jnp.logical_not(blk16), m_ab[i], 0.0))
    e2 = each(lambda i: jnp.where(jnp.logical_not(blk32), m_ab[i], 0.0))
    x = each(lambda i: eye + d1[i])
    d2 = each(lambda i: mm(d1[i], expand(d1[i])))
    ars = each(lambda i: mm_nt(ar[i], expand(s0[i])))
    kv = each(lambda i: mm(m_kk[i], v_exp[i]))
    interleave()
    x = each(lambda i: x[i] + mm(d2[i], expand(x[i])))
    d4 = each(lambda i: mm(d2[i], expand(d2[i])))
    interleave()
    x = each(lambda i: x[i] + mm(d4[i], expand(x[i])))
    d8 = each(lambda i: mm(d4[i], expand(d4[i])))
    interleave()
    x = each(lambda i: x[i] + mm(d8[i], expand(x[i])))
    t = each(lambda i: mm(e1[i], expand(x[i])))
    interleave()
    x = each(lambda i: x[i] + mm(x[i], expand(t[i])))
    t = each(lambda i: mm(e2[i], expand(x[i])))
    interleave()
    x = each(lambda i: x[i] + mm(x[i], expand(t[i])))
    interleave()
    u = each(lambda i: mm(x[i], expand(ars[i][:L] + kv[i][:L])))
    interleave()
    o = each(lambda i: ars[i][L:] + kv[i][L:] + mm(m_rb[i], expand(u[i])))
    s_add = each(lambda i: mm_tn(jnp.concatenate([vh[i], u[i].astype(BF16)], axis=0),
                                 jnp.concatenate([kh[i], bh[i]], axis=0)))
    interleave()
    for i in n:
        s_scr[i] = s0[i] * p_last[i] + s_add[i]

    seg = seg_ref[0:PW, 0:PW]
    inv_n = 1.0 / HEAD_DIM
    for i in n:
        mean = _dot1(o[i], seg) * inv_n
        cen = o[i] - mean
        var = _dot1(cen * cen, seg) * inv_n
        on = cen * lax.rsqrt(var + GN_EPS) * lng_ref[:, cs[i]] + lnb_ref[:, cs[i]]
        y_ref[items[i][0], :, cs[i]] = ((on + bonus_v[i]) * gate[i]).astype(BF16)
    while pieces:
        interleave()


def _rwkv(proj, B, T, mu, w0, wup_pad, a0, aup_pad, gup, k_k, k_a, r_k, lng, lnb, seg):
    L = CHUNK
    nc = T // L
    nseq = RWKV_SEQS if B % RWKV_SEQS == 0 else 1
    R = nseq * L
    proj3 = proj.reshape(B, T, N_MIX)
    vec = lambda n: pl.BlockSpec((1, n), lambda b, s: (0, 0))
    mat = lambda m, n: pl.BlockSpec((m, n), lambda b, s: (0, 0))
    cur = lambda s: jnp.minimum(s, nc - 1)
    y = pl.pallas_call(
        _rwkv_kernel,
        grid=(B // nseq, nc + 1),
        in_specs=[
            pl.BlockSpec((nseq, L, RWKV_IN), lambda b, s: (b, cur(s), 0)),
            pl.BlockSpec((nseq, 8, RWKV_IN), lambda b, s: (b, jnp.maximum(cur(s) * (L // 8) - 1, 0), 0)),
            vec(RWKV_IN), vec(WIDTH), mat(128, WIDTH), vec(WIDTH), mat(128, WIDTH), mat(GATE_LORA, WIDTH),
            vec(WIDTH), vec(WIDTH), vec(WIDTH), vec(WIDTH), vec(WIDTH), mat(WIDTH, WIDTH),
        ],
        out_specs=pl.BlockSpec((nseq, L, WIDTH), lambda b, s: (b, jnp.maximum(s - 1, 0), 0)),
        out_shape=jax.ShapeDtypeStruct((B, T, WIDTH), BF16),
        scratch_shapes=[pltpu.VMEM((nseq * RWKV_HEADS // PACK, HEAD_DIM, PACK * HEAD_DIM), F32)]
        + [pltpu.VMEM((R, WIDTH), BF16)] * len(RWKV_STAGED)
        + [pltpu.VMEM((R, WIDTH), F32), pltpu.VMEM((R, WIDTH), F32), pltpu.VMEM((nseq, 1, WIDTH), F32)],
        compiler_params=_params(("parallel", "arbitrary")),
        name="rwkv",
    )(proj3, proj3, mu, w0, wup_pad, a0, aup_pad, gup, k_k, k_a, r_k, lng, lnb, seg)
    return y.reshape(B * T, WIDTH)


def _bf16_split3(c):
    out, r = [], float(c)
    for _ in range(3):
        h = float(np.asarray(r, np.float32).astype(BF16).astype(np.float32))
        out.append(h)
        r -= h
    return out


LOG2E = float(np.log2(np.e))
Q_ROWS = 128
BIAS_ROW = HEAD_DIM
PEN_ROW = HEAD_DIM + 8
V_ROWS = HEAD_DIM + 16


def _attn_kernel(q_ref, k_ref, v_ref, qg_ref, kg_ref, seg_ref, place_ref, o_ref,
                 kmean_scr, k_scr, v_scr, sa_scr, sb_scr, ma_scr, mb_scr, m_scr, acc_scr):
    i = pl.program_id(1)
    nbp = kmean_scr.shape[0]
    BLK = MOBA_BLOCK

    @pl.when(i == 0)
    def _():
        kmean_scr[...] = jnp.zeros_like(kmean_scr)

    inv_n = 1.0 / HEAD_DIM
    q = q_ref[...]
    k = k_ref[...]
    half = WIDTH // 2
    seg_half = seg_ref[0:half, 0:half]
    seg_sum = lambda z: jnp.concatenate([_dot1(z[:, :half], seg_half), _dot1(z[:, half:], seg_half)], axis=1)
    qn = q * lax.rsqrt(seg_sum(q * q) * inv_n + RMS_EPS) * qg_ref[...]
    kn = k * lax.rsqrt(seg_sum(k * k) * inv_n + RMS_EPS) * kg_ref[...]
    qt = (qn * (LOG2E * HEAD_DIM ** -0.5)).T

    kmean = jnp.mean(kn, axis=0, keepdims=True)
    rown = lax.broadcasted_iota(jnp.int32, kmean_scr.shape, 0)
    kmeans = jnp.where(rown == i, kmean, kmean_scr[...])
    kmean_scr[...] = kmeans

    kaug = _dot(kn.astype(BF16), place_ref[...])
    lane = lax.broadcasted_iota(jnp.int32, kaug.shape, 1) & 127
    pos = lax.broadcasted_iota(jnp.int32, kaug.shape, 0).astype(F32)
    kaug = jnp.where((lane >= BIAS_ROW) & (lane < BIAS_ROW + 3), pos, kaug)
    kaug = jnp.where((lane >= BIAS_ROW + 3) & (lane < BIAS_ROW + 6), i.astype(F32), kaug)
    kaug = jnp.where(lane == PEN_ROW + i, 1.0, kaug).astype(BF16)

    vt = v_ref[...].T
    sub16 = lax.broadcasted_iota(jnp.int32, (V_ROWS - HEAD_DIM, BLK), 0)
    ones_rows = jnp.where(sub16 == 0, 1.0, 0.0)

    blk_id = lax.broadcasted_iota(jnp.int32, (nbp, BLK), 0)
    sub8 = lax.broadcasted_iota(jnp.int32, (8, BLK), 0)
    tail = jnp.zeros((Q_ROWS - PEN_ROW - nbp, BLK), F32)
    q = []
    for h in range(ATTN_HEADS):
        sl = slice(h * HEAD_DIM, (h + 1) * HEAD_DIM)
        qth = qt[sl, :]

        gate = _dot3(kmeans[:, sl], qth)
        gate = jnp.where(blk_id < i, gate, NEG_INF)
        chosen = blk_id == i
        for _ in range(MOBA_TOP_K):
            mx = jnp.max(gate, axis=0, keepdims=True)
            idx = jnp.min(jnp.where(gate == mx, blk_id, nbp), axis=0, keepdims=True)
            hit = blk_id == idx
            chosen = chosen | (hit & (blk_id < i))
            gate = jnp.where(hit, -jnp.inf, gate)
        penalty = jnp.where(chosen, 0.0, NEG_INF)

        slope = 2.0 ** (-8.0 * (h + 1) / ATTN_HEADS)
        cvals = _bf16_split3(slope * LOG2E) + _bf16_split3(slope * LOG2E * MOBA_BLOCK)
        consts = jnp.zeros((8, BLK), F32)
        for r, cv in enumerate(cvals):
            consts = jnp.where(sub8 == r, cv, consts)
        q.append(jnp.concatenate([qth, consts, penalty, tail], axis=0).astype(BF16))
        k_scr[h, i] = kaug[:, h * 128:(h + 1) * 128]
        v_scr[h, i] = jnp.concatenate([vt[sl, :], ones_rows], axis=0).astype(BF16)

    rowk = lax.broadcasted_iota(jnp.int32, (BLK, BLK), 0)
    colq = lax.broadcasted_iota(jnp.int32, (BLK, BLK), 1)
    causal = rowk <= colq

    heads = range(ATTN_HEADS)

    def scores_into(dst, h, j):
        s_h = _dot(k_scr[h,j], q[h])
        dst[0][h] = s_h
        dst[1][h] = jnp.max(s_h, axis=0, keepdims=True)

    def softmax_update(h, src, j):
        m_old = m_scr[h]
        m_new = jnp.maximum(m_old, src[1][h])
        alpha = jnp.exp2(m_old - m_new)
        p = jnp.exp2(src[0][h] - m_new).astype(BF16)
        m_scr[h] = m_new
        acc_scr[h] = acc_scr[h] * alpha + _dot(v_scr[h,j], p)

    def fused_step(dst, j_next, src, j):
        for h in heads:
            scores_into(dst, h, j_next)
            softmax_update(h, src, j)

    buf_a = (sa_scr, ma_scr)
    buf_b = (sb_scr, mb_scr)

    own = lambda h: _dot(k_scr[h,i], q[h])
    s_next = own(0)
    for h in heads:
        s_h = jnp.where(causal, s_next, NEG_INF)
        if h + 1 < ATTN_HEADS:
            s_next = own(h + 1)
        scores_into(buf_a, h, 0)
        m0 = jnp.max(s_h, axis=0, keepdims=True)
        m_scr[h] = m0
        acc_scr[h] = _dot(v_scr[h,i], jnp.exp2(s_h - m0).astype(BF16))

    def two_blocks(jj, carry):
        j = 2 * jj
        fused_step(buf_b, j + 1, buf_a, j)
        fused_step(buf_a, jnp.minimum(j + 2, i - 1), buf_b, j + 1)
        return carry

    lax.fori_loop(0, i // 2, two_blocks, 0)

    @pl.when(i % 2 == 1)
    def _():
        for h in heads:
            softmax_update(h, buf_a, i - 1)

    for h in heads:
        acc = acc_scr[h]
        out = acc[:HEAD_DIM] / acc[HEAD_DIM:HEAD_DIM + 1]
        o_ref[0, h * HEAD_DIM:(h + 1) * HEAD_DIM, :] = out.astype(BF16)


def _attention(proj, B, T, qg, kg, seg, place):
    nb = T // MOBA_BLOCK
    nbp = -(-nb // 8) * 8
    assert PEN_ROW + nbp <= Q_ROWS
    H = ATTN_HEADS
    col = lambda c: pl.BlockSpec((MOBA_BLOCK, WIDTH), lambda b, i: (b * nb + i, c))
    const = lambda m, n: pl.BlockSpec((m, n), lambda b, i: (0, 0))
    tile = lambda r: pltpu.VMEM((H, r, MOBA_BLOCK), F32)
    return pl.pallas_call(
        _attn_kernel,
        grid=(B, nb),
        in_specs=[col(ATTN_COL), col(ATTN_COL + 1), col(ATTN_COL + 2),
                  const(1, WIDTH), const(1, WIDTH), const(WIDTH, WIDTH), const(WIDTH, 2 * WIDTH)],
        out_specs=pl.BlockSpec((1, WIDTH, MOBA_BLOCK), lambda b, i: (b, 0, i)),
        out_shape=jax.ShapeDtypeStruct((B, WIDTH, T), BF16),
        scratch_shapes=[pltpu.VMEM((nbp, WIDTH), F32),
                        pltpu.VMEM((H, nb, MOBA_BLOCK, 128), BF16), pltpu.VMEM((H, nb, V_ROWS, MOBA_BLOCK), BF16),
                        tile(MOBA_BLOCK), tile(MOBA_BLOCK), tile(1), tile(1), tile(1), tile(V_ROWS)],
        compiler_params=_params(("parallel", "arbitrary")),
        name="moba_attention",
    )(proj, proj, proj, qg, kg, seg, place)


POOL_HALO = 16


def _merge_kernel(x_ref, yr_ref, yat_ref, pp_ref, pprev_ref, g0_ref, g1_ref, g2_ref,
                  pw_ref, pscale_ref, wb_ref, wo_ref, n2_ref, w1_ref, w2_ref, o_ref):
    t = pl.program_id(1)
    tl = x_ref.shape[0]

    u = pp_ref[...]
    halo = jnp.where(t == 0, 0.0, pprev_ref[...])
    ext = jnp.concatenate([halo, u], axis=0)
    pos = (t * tl + lax.broadcasted_iota(jnp.int32, (tl, POOL_GROUP_DIM), 0) + 1).astype(F32)
    y_pool = []
    for gi, w in enumerate(POOL_WINDOWS):
        sl = slice(gi * POOL_GROUP_DIM, (gi + 1) * POOL_GROUP_DIM)
        acc = ext[:, sl]
        s = 1
        while s < w:
            acc = acc + pltpu.roll(acc, s, axis=0)
            s *= 2
        d = acc[POOL_HALO:] / jnp.minimum(pos, float(w)) - u[:, sl]
        y_pool.append(_dot(d.astype(BF16), pw_ref[gi]))
    y_pool = jnp.concatenate(y_pool, axis=1) * pscale_ref[...]

    br_r = _dot(yr_ref[...].astype(BF16), wb_ref[0])
    br_a = _dot(yat_ref[0].astype(BF16), wb_ref[1], TN)
    br_p = _dot(y_pool.astype(BF16), wb_ref[2])
    gate = lambda ref: jax.nn.sigmoid(ref[...].astype(F32))
    merged = gate(g0_ref) * br_r + gate(g1_ref) * br_a + gate(g2_ref) * br_p
    x = x_ref[...] + _dot(merged.astype(BF16), wo_ref[...])

    ms = jnp.mean(x * x, axis=-1, keepdims=True)
    h = (x * lax.rsqrt(ms + RMS_EPS) * n2_ref[...]).astype(BF16)
    tf = D_MODEL
    for f in range(D_FF // tf):
        a = jnp.maximum(_dot(h, w1_ref[:, f * tf:(f + 1) * tf]), 0.0)
        x = x + _dot((a * a).astype(BF16), w2_ref[f * tf:(f + 1) * tf, :])
    o_ref[...] = x


def _merge(x2, y_rwkv, y_attn_t, proj, gates, B, T, pool_w, pool_scale, w_branch, w_out, norm2, w1, w2, tl=512):
    tl = min(tl, T)
    nt = T // tl
    rows = lambda w, c: pl.BlockSpec((tl, w), lambda b, t: (b * nt + t, c))
    const2 = lambda m, n: pl.BlockSpec((m, n), lambda b, t: (0, 0), pipeline_mode=pl.Buffered(1))
    const3 = lambda a, m, n: pl.BlockSpec((a, m, n), lambda b, t: (0, 0, 0), pipeline_mode=pl.Buffered(1))
    return pl.pallas_call(
        _merge_kernel,
        grid=(B, nt),
        in_specs=[
            rows(D_MODEL, 0),
            rows(WIDTH, 0),
            pl.BlockSpec((1, WIDTH, tl), lambda b, t: (b, 0, t)),
            rows(WIDTH, POOL_COL),
            pl.BlockSpec((POOL_HALO, WIDTH),
                         lambda b, t: (jnp.maximum((b * nt + t) * (tl // POOL_HALO) - 1, 0), POOL_COL)),
            rows(D_MODEL, 0), rows(D_MODEL, 1), rows(D_MODEL, 2),
            const3(len(POOL_WINDOWS), POOL_GROUP_DIM, POOL_GROUP_DIM),
            const2(1, WIDTH),
            const3(N_BRANCH, WIDTH, D_MODEL),
            const2(D_MODEL, D_MODEL),
            const2(1, D_MODEL),
            const2(D_MODEL, D_FF),
            const2(D_FF, D_MODEL),
        ],
        out_specs=rows(D_MODEL, 0),
        out_shape=jax.ShapeDtypeStruct((B * T, D_MODEL), F32),
        compiler_params=_params(("parallel", "arbitrary")),
        name="merge_mlp",
    )(x2, y_rwkv, y_attn_t, proj, proj, gates, gates, gates, pool_w, pool_scale, w_branch, w_out, norm2, w1, w2)


def _segment_ones():
    idx = np.arange(WIDTH) // HEAD_DIM
    return jnp.asarray(idx[:, None] == idx[None, :], BF16)


def _head_placement():
    src = np.arange(WIDTH)
    dst = (src // HEAD_DIM) * 128 + src % HEAD_DIM
    m = np.zeros((WIDTH, 2 * WIDTH), np.float32)
    m[src, dst] = 1.0
    return jnp.asarray(m, BF16)


def kernel(x, norm1_g, w_in, shift_mu, w0, w_up, a0, a_up, g_up, k_k, k_a, r_k, lnx_g, lnx_b,
           qn_g, kn_g, pool_w, pool_scale, w_branch, w_out, norm2_g, w_ff1, w_ff2):
    B, T, D = x.shape
    depth = w_in.shape[0]
    assert D == D_MODEL and T % MOBA_BLOCK == 0
    out_dtype = x.dtype
    x2 = x.astype(F32).reshape(B * T, D)
    seg = _segment_ones()
    place = _head_placement()
    row = lambda a: a.reshape(1, -1).astype(F32)
    zeros_lora = jnp.zeros((64, WIDTH), F32)

    for l in range(depth):
        w = w_in[l]
        w_mix = jnp.concatenate([w[:, :RWKV_IN], jnp.zeros((D, RWKV_PAD - RWKV_IN), w.dtype),
                                 w[:, RWKV_IN:RWKV_IN + 4 * WIDTH]], axis=1).astype(BF16)
        proj, gates = _inproj(x2, row(norm1_g[l]), w_mix, w[:, RWKV_IN + 4 * WIDTH:].astype(BF16))

        wup_pad = jnp.concatenate([w_up[l], zeros_lora], axis=0)
        aup_pad = jnp.concatenate([zeros_lora, a_up[l]], axis=0)
        y_rwkv = _rwkv(proj, B, T, row(shift_mu[l]), row(w0[l]), wup_pad, row(a0[l]), aup_pad, g_up[l],
                       row(k_k[l]), row(k_a[l]), row(r_k[l]), row(lnx_g[l]), row(lnx_b[l]), seg)

        y_attn_t = _attention(proj, B, T, row(jnp.tile(qn_g[l], ATTN_HEADS)),
                              row(jnp.tile(kn_g[l], ATTN_HEADS)), seg, place)

        x2 = _merge(x2, y_rwkv, y_attn_t, proj, gates, B, T, pool_w[l].astype(BF16), row(pool_scale[l]),
                    w_branch[l].astype(BF16), w_out[l].astype(BF16),
                    row(norm2_g[l]), w_ff1[l].astype(BF16), w_ff2[l].astype(BF16))

    return x2.reshape(B, T, D).astype(out_dtype)
```
